```python
import jax, jax.numpy as jnp
from jax import lax
import numpy as np

D_MODEL = 2048
BATCH = 2
SEQ = 8192
DEPTH = 2

CHUNK = 64
N_MEM = 256
EXPAND = 2
MIX_WIDTH = EXPAND * D_MODEL
W_A = MIX_WIDTH // 2
HEAD_DIM_A = 128
N_HEADS_A = W_A // HEAD_DIM_A
N_PAST_CHUNKS = 8
MAX_REL = 128
W_B = MIX_WIDTH - W_A
CONV_WIDTH = 31
GMLP_CHUNK = 128
N_GROUPS_C = 8
N_HEADS_X = 4
HEAD_DIM_X = D_MODEL // N_HEADS_X
EPS = 1e-6
N_EVEN = (DEPTH + 1) // 2
N_ODD = DEPTH // 2
AB_IN_COLS = 3 * W_A + 2 * W_B + MIX_WIDTH
C_IN_COLS = 3 * MIX_WIDTH

kernel_name = "hybrid_streaming_band_conv_sgu_encoder"


def rmsnorm(x, g):
    xf = x.astype(jnp.float32)
    y = xf * lax.rsqrt(jnp.mean(xf * xf, axis=-1, keepdims=True) + EPS)
    return (y * g.astype(jnp.float32)).astype(x.dtype)


def layernorm(x, g, b):
    xf = x.astype(jnp.float32)
    mu = jnp.mean(xf, axis=-1, keepdims=True)
    var = jnp.mean(jnp.square(xf - mu), axis=-1, keepdims=True)
    y = (xf - mu) * lax.rsqrt(var + EPS)
    return (y * g.astype(jnp.float32) + b.astype(jnp.float32)).astype(x.dtype)


def chunk_band_attention(q, k, v, rel_bias):
    b, s, h, dh = q.shape
    n_chunks = s // CHUNK
    pad = N_PAST_CHUNKS * CHUNK
    band = (N_PAST_CHUNKS + 1) * CHUNK
    k_pad = jnp.pad(k, ((0, 0), (pad, 0), (0, 0), (0, 0)))
    v_pad = jnp.pad(v, ((0, 0), (pad, 0), (0, 0), (0, 0)))
    q_off = np.arange(CHUNK)
    k_off = np.arange(band) - pad
    rel_idx = np.clip(q_off[:, None] - k_off[None, :], -MAX_REL, MAX_REL) + MAX_REL
    bias = jnp.take(rel_bias.astype(jnp.float32), jnp.asarray(rel_idx), axis=1)
    scale = dh ** -0.5
    k_off_j = jnp.asarray(k_off)

    def one_chunk(c):
        start = c * CHUNK
        qc = lax.dynamic_slice_in_dim(q, start, CHUNK, axis=1)
        kb = lax.dynamic_slice_in_dim(k_pad, start, band, axis=1)
        vb = lax.dynamic_slice_in_dim(v_pad, start, band, axis=1)
        sc = jnp.einsum('bqhd,bkhd->bhqk', qc, kb).astype(jnp.float32) * scale + bias[None]
        valid = (start + k_off_j) >= 0
        sc = jnp.where(valid[None, None, None, :], sc, jnp.float32(-1e30))
        p = jax.nn.softmax(sc, axis=-1).astype(vb.dtype)
        return jnp.einsum('bhqk,bkhd->bqhd', p, vb)

    out = lax.map(one_chunk, jnp.arange(n_chunks))
    return jnp.transpose(out, (1, 0, 2, 3, 4)).reshape(b, s, h * dh)


def causal_depthwise_conv(x, w, bias):
    c = x.shape[-1]
    xp = jnp.pad(x, ((0, 0), (w.shape[0] - 1, 0), (0, 0)))
    y = lax.conv_general_dilated(xp, w[:, None, :], window_strides=(1,), padding='VALID',
                                 dimension_numbers=('NWC', 'WIO', 'NWC'),
                                 feature_group_count=c)
    return y + bias


def ab_mixer(hn, w_in, rel_bias, conv_w, conv_b, ln_g, ln_b, w_out):
    b, s, _ = hn.shape
    proj = hn @ w_in
    splits = [W_A, 2 * W_A, 3 * W_A, 3 * W_A + W_B, 3 * W_A + 2 * W_B]
    q, k, v, glu_a, glu_b, gate = jnp.split(proj, splits, axis=-1)
    shp = (b, s, N_HEADS_A, HEAD_DIM_A)
    ya = chunk_band_attention(q.reshape(shp), k.reshape(shp), v.reshape(shp), rel_bias)
    yb = glu_a * jax.nn.sigmoid(glu_b)
    yb = jax.nn.silu(layernorm(causal_depthwise_conv(yb, conv_w, conv_b), ln_g, ln_b))
    y = jnp.concatenate([ya, yb], axis=-1) * jax.nn.silu(gate)
    return y @ w_out


def c_mixer(hn, w_in, ln_g, ln_b, w_s, b_s, w_out):
    b, s, _ = hn.shape
    u, v, gate = jnp.split(hn @ w_in, [MIX_WIDTH, 2 * MIX_WIDTH], axis=-1)
    v = layernorm(v, ln_g, ln_b)
    n_blk = s // GMLP_CHUNK
    vr = v.reshape(b, n_blk, GMLP_CHUNK, N_GROUPS_C, MIX_WIDTH // N_GROUPS_C)
    pos_chunk = np.arange(GMLP_CHUNK) // CHUNK
    mask = jnp.asarray(pos_chunk[:, None] >= pos_chunk[None, :], dtype=w_s.dtype)
    ws = w_s * mask[None]
    sg = jnp.einsum('gij,bnjgc->bnigc', ws, vr) + jnp.transpose(b_s)[None, None, :, :, None]
    y = u * sg.reshape(b, s, MIX_WIDTH) * jax.nn.silu(gate)
    return y @ w_out


def memory_cross_attention(hn, mem_n, wq, wk, wv, wo):
    b, s, _ = hn.shape
    q = (hn @ wq).reshape(b, s, N_HEADS_X, HEAD_DIM_X)
    k = (mem_n @ wk).reshape(b, N_MEM, N_HEADS_X, HEAD_DIM_X)
    v = (mem_n @ wv).reshape(b, N_MEM, N_HEADS_X, HEAD_DIM_X)
    sc = jnp.einsum('bqhd,bkhd->bhqk', q, k).astype(jnp.float32) * (HEAD_DIM_X ** -0.5)
    p = jax.nn.softmax(sc, axis=-1).astype(v.dtype)
    o = jnp.einsum('bhqk,bkhd->bqhd', p, v).reshape(b, s, D_MODEL)
    return o @ wo


def setup_inputs(seed: int = 0) -> dict:
    key = jax.random.key(seed)
    ks = iter(jax.random.split(key, 32))
    nrm = lambda shape, scale: jax.random.normal(next(ks), shape, jnp.float32) * scale
    gain = lambda shape: 1.0 + nrm(shape, 0.01)
    d = D_MODEL
    return {
        "x": nrm((BATCH, SEQ, d), 1.0),
        "mem": nrm((BATCH, N_MEM, d), 1.0),
        "norm_mix_g": gain((DEPTH, d)),
        "norm_x_g": gain((DEPTH, d)),
        "norm_mem_g": gain((DEPTH, d)),
        "final_norm_g": gain((d,)),
        "w_in_ab": nrm((N_EVEN, d, AB_IN_COLS), d ** -0.5),
        "rel_bias": nrm((N_EVEN, N_HEADS_A, 2 * MAX_REL + 1), 0.2),
        "conv_w": nrm((N_EVEN, CONV_WIDTH, W_B), CONV_WIDTH ** -0.5),
        "conv_b": nrm((N_EVEN, W_B), 0.01),
        "conv_ln_g": gain((N_EVEN, W_B)),
        "conv_ln_b": nrm((N_EVEN, W_B), 0.01),
        "w_out_ab": nrm((N_EVEN, MIX_WIDTH, d), MIX_WIDTH ** -0.5),
        "w_in_c": nrm((N_ODD, d, C_IN_COLS), d ** -0.5),
        "sgu_ln_g": gain((N_ODD, MIX_WIDTH)),
        "sgu_ln_b": nrm((N_ODD, MIX_WIDTH), 0.01),
        "w_s": nrm((N_ODD, N_GROUPS_C, GMLP_CHUNK, GMLP_CHUNK), GMLP_CHUNK ** -0.5),
        "b_s": gain((N_ODD, N_GROUPS_C, GMLP_CHUNK)),
        "w_out_c": nrm((N_ODD, MIX_WIDTH, d), MIX_WIDTH ** -0.5),
        "w_xq": nrm((DEPTH, d, d), d ** -0.5),
        "w_xk": nrm((DEPTH, d, d), d ** -0.5),
        "w_xv": nrm((DEPTH, d, d), d ** -0.5),
        "w_xo": nrm((DEPTH, d, d), d ** -0.5),
    }


def reference(x, mem, norm_mix_g, norm_x_g, norm_mem_g, final_norm_g, w_in_ab, rel_bias,
              conv_w, conv_b, conv_ln_g, conv_ln_b, w_out_ab, w_in_c, sgu_ln_g, sgu_ln_b,
              w_s, b_s, w_out_c, w_xq, w_xk, w_xv, w_xo):
    h = x
    for layer in range(DEPTH):
        i = layer // 2
        hn = rmsnorm(h, norm_mix_g[layer])
        if layer % 2 == 0:
            y = ab_mixer(hn, w_in_ab[i], rel_bias[i], conv_w[i], conv_b[i],
                         conv_ln_g[i], conv_ln_b[i], w_out_ab[i])
        else:
            y = c_mixer(hn, w_in_c[i], sgu_ln_g[i], sgu_ln_b[i], w_s[i], b_s[i], w_out_c[i])
        h = h + y
        h = h + memory_cross_attention(rmsnorm(h, norm_x_g[layer]), rmsnorm(mem, norm_mem_g[layer]),
                                       w_xq[layer], w_xk[layer], w_xv[layer], w_xo[layer])
    return rmsnorm(h, final_norm_g)
```

```python
import functools

import numpy as np
import jax
import jax.numpy as jnp
from jax import lax
from jax.experimental import pallas as pl
from jax.experimental.pallas import tpu as pltpu

F32 = jnp.float32
BF16 = jnp.bfloat16

CHUNK = 64
N_PAST_CHUNKS = 8
MAX_REL = 128
HEAD_DIM_A = 128
CONV_WIDTH = 31
GMLP_CHUNK = 128
N_GROUPS_C = 8
N_HEADS_X = 4
EPS = 1e-6
NEG_INF = -1e30

V7X_LANES = 128
V7X_VMEM_LIMIT_BYTES = 56 * 1024 * 1024

ATT_QB = 2 * CHUNK
ATT_WIN = (N_PAST_CHUNKS + 2) * CHUNK
ATT_ROLL_WIDTH = 768
ATT_N_VARIANTS = N_PAST_CHUNKS * CHUNK // ATT_QB + 1
REL_ROWS = 384
CONV_HALO = 32


def _cparams(*sem):
    return pltpu.CompilerParams(dimension_semantics=sem, vmem_limit_bytes=V7X_VMEM_LIMIT_BYTES)


def _sigmoid(x):
    return 1.0 / (1.0 + jnp.exp(-x))


def _dot(a, b):
    return jnp.dot(a, b, preferred_element_type=F32)


def _dot_nt(a, b):
    return lax.dot_general(a, b, (((1,), (1,)), ((), ())), preferred_element_type=F32)


def _norm_matmul_kernel(a_tab, b_tab, x_ref, g_ref, wa_ref, wb_ref, o_ref, hn_ref, *, segments):
    del a_tab, b_tab
    j = pl.program_id(1)
    rows_per_iter = 16

    @pl.when(j == 0)
    def _():
        g = g_ref[...]

        def body(r, carry):
            rows = pl.ds(pl.multiple_of(r * rows_per_iter, rows_per_iter), rows_per_iter)
            x = x_ref[rows, :].astype(F32)
            ms = jnp.mean(x * x, axis=-1, keepdims=True)
            hn_ref[rows, :] = ((x * lax.rsqrt(ms + EPS)) * g).astype(BF16)
            return carry

        lax.fori_loop(0, x_ref.shape[0] // rows_per_iter, body, 0)

    lo = 0
    for n_steps, kind, scale in segments:
        hi = lo + n_steps

        @pl.when((j >= lo) & (j < hi))
        def _(kind=kind, scale=scale):
            a = _dot(hn_ref[...], wa_ref[...])
            if kind == "plain":
                r = a if scale == 1.0 else a * scale
            elif kind == "silu":
                r = a * _sigmoid(a)
            elif kind == "glu":
                r = a * _sigmoid(_dot(hn_ref[...], wb_ref[...]))
            elif kind == "mulsilu":
                b = _dot(hn_ref[...], wb_ref[...])
                r = a * (b * _sigmoid(b))
            else:
                raise ValueError(kind)
            o_ref[...] = r.astype(o_ref.dtype)

        lo = hi


def _norm_matmul(x, g, w, segments, *, tm, tn):
    m, k = x.shape
    assert m % tm == 0 and w.shape[0] == k and w.shape[1] % tn == 0
    a_tab, b_tab = [], []
    for n_steps, _, _, a0, b0 in segments:
        for s in range(n_steps):
            a_tab.append(a0 + s)
            b_tab.append(None if b0 is None else b0 + s)
    known = [b for b in b_tab if b is not None]
    fill = known[0] if known else 0
    for idx, b in enumerate(b_tab):
        if b is None:
            b_tab[idx] = fill
        else:
            fill = b
    n_steps_total = len(a_tab)
    kern = functools.partial(_norm_matmul_kernel, segments=tuple(s[:3] for s in segments))
    grid_spec = pltpu.PrefetchScalarGridSpec(
        num_scalar_prefetch=2,
        grid=(m // tm, n_steps_total),
        in_specs=[
            pl.BlockSpec((tm, k), lambda i, j, at, bt: (i, 0)),
            pl.BlockSpec((1, k), lambda i, j, at, bt: (0, 0)),
            pl.BlockSpec((k, tn), lambda i, j, at, bt: (0, at[j])),
            pl.BlockSpec((k, tn), lambda i, j, at, bt: (0, bt[j])),
        ],
        out_specs=pl.BlockSpec((tm, tn), lambda i, j, at, bt: (i, j)),
        scratch_shapes=[pltpu.VMEM((tm, k), BF16)],
    )
    return pl.pallas_call(
        kern,
        grid_spec=grid_spec,
        out_shape=jax.ShapeDtypeStruct((m, n_steps_total * tn), BF16),
        compiler_params=_cparams("parallel", "arbitrary"),
        name="norm_matmul",
    )(jnp.asarray(a_tab, jnp.int32), jnp.asarray(b_tab, jnp.int32), x, g.reshape(1, k), w, w)


def _resid_matmul_kernel(l0_ref, l1_ref, w0_ref, w1_ref, r_ref, o_ref):
    acc = _dot(l0_ref[...], w0_ref[...]) + _dot(l1_ref[...], w1_ref[...])
    o_ref[...] = r_ref[...] + acc


def _resid_matmul(l0, c0, l1, c1, w, resid, *, tm, tn):
    m, n = resid.shape
    kh = w.shape[0] // 2
    assert m % tm == 0 and n % tn == 0
    return pl.pallas_call(
        _resid_matmul_kernel,
        grid=(m // tm, n // tn),
        in_specs=[
            pl.BlockSpec((tm, kh), lambda i, j: (i, c0)),
            pl.BlockSpec((tm, kh), lambda i, j: (i, c1)),
            pl.BlockSpec((kh, tn), lambda i, j: (0, j)),
            pl.BlockSpec((kh, tn), lambda i, j: (1, j)),
            pl.BlockSpec((tm, tn), lambda i, j: (i, j)),
        ],
        out_specs=pl.BlockSpec((tm, tn), lambda i, j: (i, j)),
        out_shape=jax.ShapeDtypeStruct((m, n), F32),
        compiler_params=_cparams("parallel", "arbitrary"),
        name="resid_matmul",
    )(l0, l1, w, w, resid)


def _bias_table_kernel(rb_ref, o_ref, *, n_heads):
    t = pl.program_id(0)
    width = ATT_ROLL_WIDTH
    m = lax.broadcasted_iota(jnp.int32, (REL_ROWS, width), 1)
    r = lax.broadcasted_iota(jnp.int32, (REL_ROWS, width), 0)
    mm = jnp.where(m < ATT_WIN, m, m - width)
    idx = jnp.clip(ATT_QB * t - mm, -MAX_REL, MAX_REL) + MAX_REL
    onehot = jnp.where(r == idx, 1.0, 0.0).astype(BF16)
    rb = rb_ref[...]
    hi = rb.astype(BF16)
    rem = rb - hi.astype(F32)
    mid = rem.astype(BF16)
    lo = (rem - mid.astype(F32)).astype(BF16)
    u = _dot(hi, onehot) + _dot(mid, onehot) + _dot(lo, onehot)

    qi = lax.broadcasted_iota(jnp.int32, (ATT_QB, ATT_WIN), 0) // CHUNK
    kj = lax.broadcasted_iota(jnp.int32, (ATT_QB, ATT_WIN), 1) // CHUNK
    q_chunk = (ATT_QB // CHUNK) * t + qi
    valid = (kj <= q_chunk) & (kj >= q_chunk - N_PAST_CHUNKS)
    for h in range(n_heads):
        x = jnp.broadcast_to(u[h:h + 1, :], (ATT_QB, width))
        toeplitz = pltpu.roll(x, 0, 1, stride=1, stride_axis=0)
        o_ref[0, h] = jnp.where(valid, toeplitz[:, :ATT_WIN], NEG_INF)


def _bias_tables(rel_bias):
    n_heads, n_rel = rel_bias.shape
    rb = jnp.pad(rel_bias.astype(F32), ((0, 0), (0, REL_ROWS - n_rel)))
    return pl.pallas_call(
        functools.partial(_bias_table_kernel, n_heads=n_heads),
        grid=(ATT_N_VARIANTS,),
        in_specs=[pl.BlockSpec((n_heads, REL_ROWS), lambda t: (0, 0))],
        out_specs=pl.BlockSpec((1, n_heads, ATT_QB, ATT_WIN), lambda t: (t, 0, 0, 0)),
        out_shape=jax.ShapeDtypeStruct((ATT_N_VARIANTS, n_heads, ATT_QB, ATT_WIN), F32),
        compiler_params=_cparams("arbitrary"),
        name="bias_tables",
    )(rb)


def _band_attn_kernel(q_ref, k_ref, v_ref, bias_ref, gate_ref, o_ref, *, heads_per_step, q_blocks):
    s_idx = pl.program_id(2)
    dh = HEAD_DIM_A
    last_variant = ATT_N_VARIANTS - 1

    def body(qb, carry):
        blk = s_idx * q_blocks + qb
        variant = jnp.minimum(blk, last_variant)
        start = pl.multiple_of(jnp.maximum(blk - last_variant, 0) * ATT_QB, ATT_QB)
        rows = pl.ds(pl.multiple_of(qb * ATT_QB, ATT_QB), ATT_QB)
        for h in range(heads_per_step):
            cols = slice(h * dh, (h + 1) * dh)
            q = q_ref[rows, cols]
            kw = k_ref[pl.ds(start, ATT_WIN), cols]
            vw = v_ref[pl.ds(start, ATT_WIN), cols]
            s = _dot_nt(q, kw) + bias_ref[variant, h]
            p = jnp.exp(s - jnp.max(s, axis=-1, keepdims=True))
            denom = jnp.sum(p, axis=-1, keepdims=True)
            o = _dot(p.astype(BF16), vw) / denom
            o_ref[rows, cols] = (o * gate_ref[rows, cols].astype(F32)).astype(o_ref.dtype)
        return carry

    lax.fori_loop(0, q_blocks, body, 0)


def _band_attention(pa, bias, *, batch, seq, w_a, gate_col):
    heads_per_step = 2
    cw = heads_per_step * HEAD_DIM_A
    q_blocks = 8
    rows = q_blocks * ATT_QB
    assert seq % rows == 0 and w_a % cw == 0 and gate_col % cw == 0
    n_s = seq // rows
    kern = functools.partial(_band_attn_kernel, heads_per_step=heads_per_step, q_blocks=q_blocks)
    return pl.pallas_call(
        kern,
        grid=(batch, w_a // cw, n_s),
        in_specs=[
            pl.BlockSpec((rows, cw), lambda b, g, s: (b * n_s + s, g)),
            pl.BlockSpec((seq, cw), lambda b, g, s: (b, w_a // cw + g)),
            pl.BlockSpec((seq, cw), lambda b, g, s: (b, 2 * (w_a // cw) + g)),
            pl.BlockSpec((ATT_N_VARIANTS, heads_per_step, ATT_QB, ATT_WIN), lambda b, g, s: (0, g, 0, 0)),
            pl.BlockSpec((rows, cw), lambda b, g, s: (b * n_s + s, gate_col // cw + g)),
        ],
        out_specs=pl.BlockSpec((rows, cw), lambda b, g, s: (b * n_s + s, g)),
        out_shape=jax.ShapeDtypeStruct((batch * seq, w_a), BF16),
        compiler_params=_cparams("parallel", "parallel", "arbitrary"),
        name="band_attention",
    )(pa, pa, pa, bias, pa)


def _conv_module_kernel(x_ref, halo_ref, w_ref, cb_ref, lg_ref, lb_ref, gate_ref, o_ref, xw_ref, y_ref):
    ts, c = x_ref.shape
    i = pl.program_id(1)
    n_lane_blocks = c // V7X_LANES
    for lb in range(n_lane_blocks):
        cols = slice(lb * V7X_LANES, (lb + 1) * V7X_LANES)
        xw_ref[lb, CONV_HALO:CONV_HALO + ts, :] = x_ref[:, cols].astype(F32)

    @pl.when(i == 0)
    def _():
        xw_ref[:, 0:CONV_HALO, :] = jnp.zeros((n_lane_blocks, CONV_HALO, V7X_LANES), F32)

    @pl.when(i > 0)
    def _():
        for lb in range(n_lane_blocks):
            cols = slice(lb * V7X_LANES, (lb + 1) * V7X_LANES)
            xw_ref[lb, 0:CONV_HALO, :] = halo_ref[:, cols].astype(F32)

    row_blk = 128
    sub = 8
    first_tap = CONV_HALO - (CONV_WIDTH - 1)

    def conv_body(it, carry):
        rb = it // n_lane_blocks
        lb = it % n_lane_blocks
        cols = pl.ds(pl.multiple_of(lb * V7X_LANES, V7X_LANES), V7X_LANES)
        base = rb * row_blk
        acc = [jnp.broadcast_to(cb_ref[:, cols], (sub, V7X_LANES))] * (row_blk // sub)
        for tap in range(CONV_WIDTH):
            wk = jnp.broadcast_to(w_ref[tap:tap + 1, cols], (sub, V7X_LANES))
            acc = [a + wk * xw_ref[lb, pl.ds(base + r * sub + first_tap + tap, sub), :]
                   for r, a in enumerate(acc)]
        for r, a in enumerate(acc):
            y_ref[pl.ds(pl.multiple_of(base + r * sub, sub), sub), cols] = a
        return carry

    lax.fori_loop(0, (ts // row_blk) * n_lane_blocks, conv_body, 0)

    ln_rows = 16

    def ln_body(r, carry):
        rows = pl.ds(pl.multiple_of(r * ln_rows, ln_rows), ln_rows)
        y = y_ref[rows, :]
        mu = jnp.mean(y, axis=-1, keepdims=True)
        d = y - mu
        var = jnp.mean(d * d, axis=-1, keepdims=True)
        z = (d * lax.rsqrt(var + EPS)) * lg_ref[...] + lb_ref[...]
        z = z * _sigmoid(z)
        o_ref[rows, :] = (z * gate_ref[rows, :].astype(F32)).astype(o_ref.dtype)
        return carry

    lax.fori_loop(0, ts // ln_rows, ln_body, 0)


def _conv_module(pa, conv_w, conv_b, ln_g, ln_b, *, batch, seq, w_b, x_col, gate_col, ts):
    assert seq % ts == 0 and ts % 128 == 0 and ts % CONV_HALO == 0
    n_s = seq // ts
    halo_per_blk = ts // CONV_HALO
    return pl.pallas_call(
        _conv_module_kernel,
        grid=(batch, n_s),
        in_specs=[
            pl.BlockSpec((ts, w_b), lambda b, i: (b * n_s + i, x_col)),
            pl.BlockSpec((CONV_HALO, w_b),
                         lambda b, i: (jnp.maximum((b * n_s + i) * halo_per_blk - 1, 0), x_col)),
            pl.BlockSpec((CONV_WIDTH, w_b), lambda b, i: (0, 0)),
            pl.BlockSpec((1, w_b), lambda b, i: (0, 0)),
            pl.BlockSpec((1, w_b), lambda b, i: (0, 0)),
            pl.BlockSpec((1, w_b), lambda b, i: (0, 0)),
            pl.BlockSpec((ts, w_b), lambda b, i: (b * n_s + i, gate_col)),
        ],
        out_specs=pl.BlockSpec((ts, w_b), lambda b, i: (b * n_s + i, 0)),
        out_shape=jax.ShapeDtypeStruct((batch * seq, w_b), BF16),
        scratch_shapes=[pltpu.VMEM((w_b // V7X_LANES, CONV_HALO + ts, V7X_LANES), F32),
                        pltpu.VMEM((ts, w_b), F32)],
        compiler_params=_cparams("parallel", "arbitrary"),
        name="conv_module",
    )(pa, pa, conv_w.astype(F32), conv_b.reshape(1, w_b).astype(F32), ln_g.reshape(1, w_b).astype(F32),
      ln_b.reshape(1, w_b).astype(F32), pa)


def _sgu_kernel(ug_ref, v_ref, lg_ref, lb_ref, ws_ref, bs_ref, o_ref):
    tb, width = v_ref.shape
    gw = width // N_GROUPS_C
    pos_r = lax.broadcasted_iota(jnp.int32, (GMLP_CHUNK, GMLP_CHUNK), 0) // CHUNK
    pos_c = lax.broadcasted_iota(jnp.int32, (GMLP_CHUNK, GMLP_CHUNK), 1) // CHUNK
    causal = pos_r >= pos_c
    for n in range(tb // GMLP_CHUNK):
        rows = slice(n * GMLP_CHUNK, (n + 1) * GMLP_CHUNK)
        v = v_ref[rows, :].astype(F32)
        mu = jnp.mean(v, axis=-1, keepdims=True)
        d = v - mu
        var = jnp.mean(d * d, axis=-1, keepdims=True)
        vn = ((d * lax.rsqrt(var + EPS)) * lg_ref[...] + lb_ref[...]).astype(BF16)
        for g in range(N_GROUPS_C):
            cols = slice(g * gw, (g + 1) * gw)
            ws = jnp.where(causal, ws_ref[g], 0.0).astype(BF16)
            sg = _dot(ws, vn[:, cols]) + bs_ref[g]
            o_ref[rows, cols] = (ug_ref[rows, cols].astype(F32) * sg).astype(o_ref.dtype)


def _sgu(uv, ln_g, ln_b, w_s, b_s, *, width, tb):
    m = uv.shape[0]
    assert m % tb == 0 and tb % GMLP_CHUNK == 0
    return pl.pallas_call(
        _sgu_kernel,
        grid=(m // tb,),
        in_specs=[
            pl.BlockSpec((tb, width), lambda i: (i, 0)),
            pl.BlockSpec((tb, width), lambda i: (i, 1)),
            pl.BlockSpec((1, width), lambda i: (0, 0)),
            pl.BlockSpec((1, width), lambda i: (0, 0)),
            pl.BlockSpec((N_GROUPS_C, GMLP_CHUNK, GMLP_CHUNK), lambda i: (0, 0, 0)),
            pl.BlockSpec((N_GROUPS_C, GMLP_CHUNK, 1), lambda i: (0, 0, 0)),
        ],
        out_specs=pl.BlockSpec((tb, width), lambda i: (i, 0)),
        out_shape=jax.ShapeDtypeStruct((m, width), BF16),
        compiler_params=_cparams("parallel"),
        name="sgu",
    )(uv, uv, ln_g.reshape(1, width).astype(F32), ln_b.reshape(1, width).astype(F32),
      w_s.astype(F32), b_s.astype(F32)[..., None])


def _xattn_kernel(q_ref, k_ref, v_ref, wo_ref, r_ref, fg_ref, o_ref, ctx_ref, *, final_norm):
    d = q_ref.shape[1]
    dh = d // N_HEADS_X
    for h in range(N_HEADS_X):
        cols = slice(h * dh, (h + 1) * dh)
        s = _dot_nt(q_ref[:, cols], k_ref[:, cols])
        p = jnp.exp(s - jnp.max(s, axis=-1, keepdims=True))
        denom = jnp.sum(p, axis=-1, keepdims=True)
        ctx_ref[:, cols] = (_dot(p.astype(BF16), v_ref[:, cols]) / denom).astype(BF16)
    h_new = r_ref[...] + _dot(ctx_ref[...], wo_ref[...])
    if final_norm:
        ms = jnp.mean(h_new * h_new, axis=-1, keepdims=True)
        h_new = (h_new * lax.rsqrt(ms + EPS)) * fg_ref[...]
    o_ref[...] = h_new


def _cross_attention(qx, kv, wo, resid, final_g, *, batch, seq, n_mem, tq, final_norm):
    m, d = resid.shape
    assert seq % tq == 0
    n_s = seq // tq
    kern = functools.partial(_xattn_kernel, final_norm=final_norm)
    return pl.pallas_call(
        kern,
        grid=(batch, n_s),
        in_specs=[
            pl.BlockSpec((tq, d), lambda b, i: (b * n_s + i, 0)),
            pl.BlockSpec((n_mem, d), lambda b, i: (b, 0)),
            pl.BlockSpec((n_mem, d), lambda b, i: (b, 1)),
            pl.BlockSpec((d, d), lambda b, i: (0, 0)),
            pl.BlockSpec((tq, d), lambda b, i: (b * n_s + i, 0)),
            pl.BlockSpec((1, d), lambda b, i: (0, 0)),
        ],
        out_specs=pl.BlockSpec((tq, d), lambda b, i: (b * n_s + i, 0)),
        out_shape=jax.ShapeDtypeStruct((m, d), F32),
        scratch_shapes=[pltpu.VMEM((tq, d), BF16)],
        compiler_params=_cparams("parallel", "arbitrary"),
        name="cross_attention",
    )(qx, kv, kv, wo, resid, final_g.reshape(1, d).astype(F32))


def _pick(n, *cands):
    for c in cands:
        if n % c == 0:
            return c
    raise ValueError(f"no tile for {n} among {cands}")


def kernel(x, mem, norm_mix_g, norm_x_g, norm_mem_g, final_norm_g, w_in_ab, rel_bias, conv_w, conv_b,
           conv_ln_g, conv_ln_b, w_out_ab, w_in_c, sgu_ln_g, sgu_ln_b, w_s, b_s, w_out_c, w_xq, w_xk,
           w_xv, w_xo):
    batch, seq, d = x.shape
    n_mem = mem.shape[1]
    depth = norm_mix_g.shape[0]
    mix = w_out_ab.shape[1]
    w_a = mix // 2
    w_b = mix - w_a
    t = batch * seq
    tm = _pick(t, 1024, 512, 256)
    tn = _pick(w_a, 512, 256)
    nb_a, nb_b, nb_mix, nb_d = w_a // tn, w_b // tn, mix // tn, d // tn
    scale_a = HEAD_DIM_A ** -0.5
    scale_x = (d // N_HEADS_X) ** -0.5

    h = x.reshape(t, d).astype(F32)
    mem2 = mem.reshape(batch * n_mem, d).astype(F32)
    for layer in range(depth):
        i = layer // 2
        if layer % 2 == 0:
            segs = [
                (nb_a, "plain", scale_a, 0, None),
                (2 * nb_a, "plain", 1.0, nb_a, None),
                (nb_b, "glu", 1.0, 3 * nb_a, 3 * nb_a + nb_b),
                (nb_mix, "silu", 1.0, 3 * nb_a + 2 * nb_b, None),
            ]
            pa = _norm_matmul(h, norm_mix_g[layer], w_in_ab[i].astype(BF16), segs, tm=tm, tn=tn)
            gate_col = 3 * w_a + w_b
            bias = _bias_tables(rel_bias[i])
            ya = _band_attention(pa, bias, batch=batch, seq=seq, w_a=w_a, gate_col=gate_col)
            yb = _conv_module(pa, conv_w[i], conv_b[i], conv_ln_g[i], conv_ln_b[i], batch=batch, seq=seq,
                              w_b=w_b, x_col=3 * w_a // w_b, gate_col=(gate_col + w_a) // w_b,
                              ts=_pick(seq, 256, 128))
            h = _resid_matmul(ya, 0, yb, 0, w_out_ab[i].astype(BF16), h, tm=tm, tn=tn)
        else:
            segs = [
                (nb_mix, "mulsilu", 1.0, 0, 2 * nb_mix),
                (nb_mix, "plain", 1.0, nb_mix, None),
            ]
            uv = _norm_matmul(h, norm_mix_g[layer], w_in_c[i].astype(BF16), segs, tm=tm, tn=tn)
            y = _sgu(uv, sgu_ln_g[i], sgu_ln_b[i], w_s[i], b_s[i], width=mix, tb=_pick(t, 256, 128))
            h = _resid_matmul(y, 0, y, 1, w_out_c[i].astype(BF16), h, tm=tm, tn=tn)

        w_kv = jnp.concatenate([w_xk[layer], w_xv[layer]], axis=1).astype(BF16)
        kv = _norm_matmul(mem2, norm_mem_g[layer], w_kv, [(2 * nb_d, "plain", 1.0, 0, None)],
                          tm=_pick(batch * n_mem, 512, 256, 128), tn=tn)
        qx = _norm_matmul(h, norm_x_g[layer], w_xq[layer].astype(BF16), [(nb_d, "plain", scale_x, 0, None)],
                          tm=tm, tn=tn)
        h = _cross_attention(qx, kv, w_xo[layer].astype(BF16), h, final_norm_g, batch=batch, seq=seq,
                             n_mem=n_mem, tq=_pick(seq, 256, 128), final_norm=(layer == depth - 1))
    return h.reshape(batch, seq, d)
```

```python
import functools

import numpy as np
import jax
import jax.numpy as jnp
from jax import lax
from jax.experimental import pallas as pl
from jax.experimental.pallas import tpu as pltpu

F32 = jnp.float32
BF16 = jnp.bfloat16

CHUNK = 64
N_PAST_CHUNKS = 8
MAX_REL = 128
HEAD_DIM_A = 128
CONV_WIDTH = 31
GMLP_CHUNK = 128
N_GROUPS_C = 8
N_HEADS_X = 4
EPS = 1e-6
NEG_INF = -1e30

V7X_LANES = 128
V7X_VMEM_LIMIT_BYTES = 56 * 1024 * 1024

ATT_QB = 2 * CHUNK
ATT_WIN = (N_PAST_CHUNKS + 2) * CHUNK
ATT_ROLL_WIDTH = 768
ATT_N_VARIANTS = N_PAST_CHUNKS * CHUNK // ATT_QB + 1
REL_ROWS = 384
CONV_HALO = 32


def _cparams(*sem):
    return pltpu.CompilerParams(dimension_semantics=sem, vmem_limit_bytes=V7X_VMEM_LIMIT_BYTES)


def _sigmoid(x):
    return 1.0 / (1.0 + jnp.exp(-x))


def _dot(a, b):
    return jnp.dot(a, b, preferred_element_type=F32)


def _dot_nt(a, b):
    return lax.dot_general(a, b, (((1,), (1,)), ((), ())), preferred_element_type=F32)


def _norm_matmul_kernel(a_tab, b_tab, x_ref, g_ref, wa_ref, wb_ref, o_ref, hn_ref, *, segments):
    del a_tab, b_tab
    j = pl.program_id(1)
    rows_per_iter = 16

    @pl.when(j == 0)
    def _():
        g = g_ref[...]

        def body(r, carry):
            rows = pl.ds(pl.multiple_of(r * rows_per_iter, rows_per_iter), rows_per_iter)
            x = x_ref[rows, :].astype(F32)
            ms = jnp.mean(x * x, axis=-1, keepdims=True)
            hn_ref[rows, :] = ((x * lax.rsqrt(ms + EPS)) * g).astype(BF16)
            return carry

        lax.fori_loop(0, x_ref.shape[0] // rows_per_iter, body, 0, unroll=4)

    lo = 0
    for n_steps, kind, scale in segments:
        hi = lo + n_steps

        @pl.when((j >= lo) & (j < hi))
        def _(kind=kind, scale=scale):
            a = _dot(hn_ref[...], wa_ref[...])
            if kind == "plain":
                r = a if scale == 1.0 else a * scale
            elif kind == "silu":
                r = a * _sigmoid(a)
            elif kind == "glu":
                r = a * _sigmoid(_dot(hn_ref[...], wb_ref[...]))
            elif kind == "mulsilu":
                b = _dot(hn_ref[...], wb_ref[...])
                r = a * (b * _sigmoid(b))
            else:
                raise ValueError(kind)
            o_ref[...] = r.astype(o_ref.dtype)

        lo = hi


def _norm_matmul(x, g, w, segments, *, tm, tn):
    m, k = x.shape
    assert m % tm == 0 and w.shape[0] == k and w.shape[1] % tn == 0
    a_tab, b_tab = [], []
    for n_steps, _, _, a0, b0 in segments:
        for s in range(n_steps):
            a_tab.append(a0 + s)
            b_tab.append(None if b0 is None else b0 + s)
    known = [b for b in b_tab if b is not None]
    fill = known[0] if known else 0
    for idx, b in enumerate(b_tab):
        if b is None:
            b_tab[idx] = fill
        else:
            fill = b
    n_steps_total = len(a_tab)
    kern = functools.partial(_norm_matmul_kernel, segments=tuple(s[:3] for s in segments))
    grid_spec = pltpu.PrefetchScalarGridSpec(
        num_scalar_prefetch=2,
        grid=(m // tm, n_steps_total),
        in_specs=[
            pl.BlockSpec((tm, k), lambda i, j, at, bt: (i, 0)),
            pl.BlockSpec((1, k), lambda i, j, at, bt: (0, 0)),
            pl.BlockSpec((k, tn), lambda i, j, at, bt: (0, at[j])),
            pl.BlockSpec((k, tn), lambda i, j, at, bt: (0, bt[j])),
        ],
        out_specs=pl.BlockSpec((tm, tn), lambda i, j, at, bt: (i, j)),
        scratch_shapes=[pltpu.VMEM((tm, k), BF16)],
    )
    return pl.pallas_call(
        kern,
        grid_spec=grid_spec,
        out_shape=jax.ShapeDtypeStruct((m, n_steps_total * tn), BF16),
        compiler_params=_cparams("parallel", "arbitrary"),
        name="norm_matmul",
    )(jnp.asarray(a_tab, jnp.int32), jnp.asarray(b_tab, jnp.int32), x, g.reshape(1, k), w, w)


def _resid_matmul_kernel(l0_ref, l1_ref, w0_ref, w1_ref, r_ref, o_ref):
    acc = _dot(l0_ref[...], w0_ref[...]) + _dot(l1_ref[...], w1_ref[...])
    o_ref[...] = r_ref[...] + acc


def _resid_matmul(l0, c0, l1, c1, w, resid, *, tm, tn):
    m, n = resid.shape
    kh = w.shape[0] // 2
    assert m % tm == 0 and n % tn == 0
    return pl.pallas_call(
        _resid_matmul_kernel,
        grid=(m // tm, n // tn),
        in_specs=[
            pl.BlockSpec((tm, kh), lambda i, j: (i, c0)),
            pl.BlockSpec((tm, kh), lambda i, j: (i, c1)),
            pl.BlockSpec((kh, tn), lambda i, j: (0, j)),
            pl.BlockSpec((kh, tn), lambda i, j: (1, j)),
            pl.BlockSpec((tm, tn), lambda i, j: (i, j)),
        ],
        out_specs=pl.BlockSpec((tm, tn), lambda i, j: (i, j)),
        out_shape=jax.ShapeDtypeStruct((m, n), F32),
        compiler_params=_cparams("parallel", "arbitrary"),
        name="resid_matmul",
    )(l0, l1, w, w, resid)


def _bias_table_kernel(rb_ref, o_ref, *, n_heads):
    t = pl.program_id(0)
    width = ATT_ROLL_WIDTH
    m = lax.broadcasted_iota(jnp.int32, (REL_ROWS, width), 1)
    r = lax.broadcasted_iota(jnp.int32, (REL_ROWS, width), 0)
    mm = jnp.where(m < ATT_WIN, m, m - width)
    idx = jnp.clip(ATT_QB * t - mm, -MAX_REL, MAX_REL) + MAX_REL
    onehot = jnp.where(r == idx, 1.0, 0.0).astype(BF16)
    rb = rb_ref[...]
    hi = rb.astype(BF16)
    rem = rb - hi.astype(F32)
    mid = rem.astype(BF16)
    lo = (rem - mid.astype(F32)).astype(BF16)
    u = _dot(hi, onehot) + _dot(mid, onehot) + _dot(lo, onehot)

    qi = lax.broadcasted_iota(jnp.int32, (ATT_QB, ATT_WIN), 0) // CHUNK
    kj = lax.broadcasted_iota(jnp.int32, (ATT_QB, ATT_WIN), 1) // CHUNK
    q_chunk = (ATT_QB // CHUNK) * t + qi
    valid = (kj <= q_chunk) & (kj >= q_chunk - N_PAST_CHUNKS)
    for h in range(n_heads):
        x = jnp.broadcast_to(u[h:h + 1, :], (ATT_QB, width))
        toeplitz = pltpu.roll(x, 0, 1, stride=1, stride_axis=0)
        o_ref[0, h] = jnp.where(valid, toeplitz[:, :ATT_WIN], NEG_INF)


def _bias_tables(rel_bias):
    n_heads, n_rel = rel_bias.shape
    rb = jnp.pad(rel_bias.astype(F32), ((0, 0), (0, REL_ROWS - n_rel)))
    return pl.pallas_call(
        functools.partial(_bias_table_kernel, n_heads=n_heads),
        grid=(ATT_N_VARIANTS,),
        in_specs=[pl.BlockSpec((n_heads, REL_ROWS), lambda t: (0, 0))],
        out_specs=pl.BlockSpec((1, n_heads, ATT_QB, ATT_WIN), lambda t: (t, 0, 0, 0)),
        out_shape=jax.ShapeDtypeStruct((ATT_N_VARIANTS, n_heads, ATT_QB, ATT_WIN), F32),
        compiler_params=_cparams("arbitrary"),
        name="bias_tables",
    )(rb)


def _band_attn_kernel(q_ref, k_ref, v_ref, bias_ref, gate_ref, o_ref,
                      s_even, s_odd, p_even, p_odd, l_even, l_odd, *, heads_per_step):
    dh = HEAD_DIM_A
    n_blocks = q_ref.shape[0] // ATT_QB
    last_variant = ATT_N_VARIANTS - 1
    heads = [(h, slice(h * dh, (h + 1) * dh)) for h in range(heads_per_step)]

    def window(blk):
        return pl.ds(pl.multiple_of(jnp.maximum(blk - last_variant, 0) * ATT_QB, ATT_QB), ATT_WIN)

    def block_rows(blk):
        return pl.ds(pl.multiple_of(blk * ATT_QB, ATT_QB), ATT_QB)

    def scores(blk, s_buf):
        variant = jnp.minimum(blk, last_variant)
        for h, cols in heads:
            s_buf[h] = _dot_nt(q_ref[block_rows(blk), cols], k_ref[window(blk), cols]) + bias_ref[variant, h]

    def exponentiate(s_buf, p_buf, l_buf):
        for h, _ in heads:
            s = s_buf[h]
            p = jnp.exp(s - jnp.max(s, axis=-1, keepdims=True))
            l_buf[h] = jnp.sum(p, axis=-1, keepdims=True)
            p_buf[h] = p.astype(BF16)

    def apply_values(blk, p_buf, l_buf):
        rows = block_rows(blk)
        for h, cols in heads:
            o = _dot(p_buf[h], v_ref[window(blk), cols]) / l_buf[h]
            o_ref[rows, cols] = (o * gate_ref[rows, cols].astype(F32)).astype(o_ref.dtype)

    scores(0, s_even)
    scores(1, s_odd)
    exponentiate(s_even, p_even, l_even)

    def pair(t, carry):
        m = 2 * t + 1
        scores(m + 1, s_even)
        exponentiate(s_odd, p_odd, l_odd)
        apply_values(m - 1, p_even, l_even)
        scores(m + 2, s_odd)
        exponentiate(s_even, p_even, l_even)
        apply_values(m, p_odd, l_odd)
        return carry

    lax.fori_loop(0, (n_blocks - 2) // 2, pair, 0)
    exponentiate(s_odd, p_odd, l_odd)
    apply_values(n_blocks - 2, p_even, l_even)
    apply_values(n_blocks - 1, p_odd, l_odd)


def _band_attention(pa, bias, *, batch, seq, w_a, gate_col):
    heads_per_step = 2
    cw = heads_per_step * HEAD_DIM_A
    n_blocks = seq // ATT_QB
    assert seq % ATT_QB == 0 and n_blocks % 2 == 0 and n_blocks >= 4
    assert w_a % cw == 0 and gate_col % cw == 0
    kern = functools.partial(_band_attn_kernel, heads_per_step=heads_per_step)
    seq_cols = lambda first: pl.BlockSpec((seq, cw), lambda b, g: (b, first // cw + g))
    return pl.pallas_call(
        kern,
        grid=(batch, w_a // cw),
        in_specs=[
            seq_cols(0),
            seq_cols(w_a),
            seq_cols(2 * w_a),
            pl.BlockSpec((ATT_N_VARIANTS, heads_per_step, ATT_QB, ATT_WIN), lambda b, g: (0, g, 0, 0)),
            seq_cols(gate_col),
        ],
        out_specs=seq_cols(0),
        out_shape=jax.ShapeDtypeStruct((batch * seq, w_a), BF16),
        scratch_shapes=[
            pltpu.VMEM((heads_per_step, ATT_QB, ATT_WIN), F32),
            pltpu.VMEM((heads_per_step, ATT_QB, ATT_WIN), F32),
            pltpu.VMEM((heads_per_step, ATT_QB, ATT_WIN), BF16),
            pltpu.VMEM((heads_per_step, ATT_QB, ATT_WIN), BF16),
            pltpu.VMEM((heads_per_step, ATT_QB, 1), F32),
            pltpu.VMEM((heads_per_step, ATT_QB, 1), F32),
        ],
        compiler_params=_cparams("parallel", "parallel"),
        name="band_attention",
    )(pa, pa, pa, bias, pa)


def _conv_module_kernel(x_ref, halo_ref, w_ref, cb_ref, lg_ref, lb_ref, gate_ref, o_ref, xw_ref, y_ref):
    ts, c = x_ref.shape
    i = pl.program_id(1)
    n_lane_blocks = c // V7X_LANES
    for lb in range(n_lane_blocks):
        cols = slice(lb * V7X_LANES, (lb + 1) * V7X_LANES)
        xw_ref[lb, CONV_HALO:CONV_HALO + ts, :] = x_ref[:, cols].astype(F32)

    @pl.when(i == 0)
    def _():
        xw_ref[:, 0:CONV_HALO, :] = jnp.zeros((n_lane_blocks, CONV_HALO, V7X_LANES), F32)

    @pl.when(i > 0)
    def _():
        for lb in range(n_lane_blocks):
            cols = slice(lb * V7X_LANES, (lb + 1) * V7X_LANES)
            xw_ref[lb, 0:CONV_HALO, :] = halo_ref[:, cols].astype(F32)

    row_blk = 128
    sub = 8
    first_tap = CONV_HALO - (CONV_WIDTH - 1)

    def conv_body(it, carry):
        rb = it // n_lane_blocks
        lb = it % n_lane_blocks
        cols = pl.ds(pl.multiple_of(lb * V7X_LANES, V7X_LANES), V7X_LANES)
        base = rb * row_blk
        acc = [jnp.broadcast_to(cb_ref[:, cols], (sub, V7X_LANES))] * (row_blk // sub)
        for tap in range(CONV_WIDTH):
            wk = jnp.broadcast_to(w_ref[tap:tap + 1, cols], (sub, V7X_LANES))
            acc = [a + wk * xw_ref[lb, pl.ds(base + r * sub + first_tap + tap, sub), :]
                   for r, a in enumerate(acc)]
        for r, a in enumerate(acc):
            y_ref[pl.ds(pl.multiple_of(base + r * sub, sub), sub), cols] = a
        return carry

    lax.fori_loop(0, (ts // row_blk) * n_lane_blocks, conv_body, 0)

    ln_rows = 16

    def ln_body(r, carry):
        rows = pl.ds(pl.multiple_of(r * ln_rows, ln_rows), ln_rows)
        y = y_ref[rows, :]
        mu = jnp.mean(y, axis=-1, keepdims=True)
        d = y - mu
        var = jnp.mean(d * d, axis=-1, keepdims=True)
        z = (d * lax.rsqrt(var + EPS)) * lg_ref[...] + lb_ref[...]
        z = z * _sigmoid(z)
        o_ref[rows, :] = (z * gate_ref[rows, :].astype(F32)).astype(o_ref.dtype)
        return carry

    lax.fori_loop(0, ts // ln_rows, ln_body, 0, unroll=4)


def _conv_module(pa, conv_w, conv_b, ln_g, ln_b, *, batch, seq, w_b, x_col, gate_col, ts):
    assert seq % ts == 0 and ts % 128 == 0 and ts % CONV_HALO == 0
    n_s = seq // ts
    halo_per_blk = ts // CONV_HALO
    return pl.pallas_call(
        _conv_module_kernel,
        grid=(batch, n_s),
        in_specs=[
            pl.BlockSpec((ts, w_b), lambda b, i: (b * n_s + i, x_col)),
            pl.BlockSpec((CONV_HALO, w_b),
                         lambda b, i: (jnp.maximum((b * n_s + i) * halo_per_blk - 1, 0), x_col)),
            pl.BlockSpec((CONV_WIDTH, w_b), lambda b, i: (0, 0)),
            pl.BlockSpec((1, w_b), lambda b, i: (0, 0)),
            pl.BlockSpec((1, w_b), lambda b, i: (0, 0)),
            pl.BlockSpec((1, w_b), lambda b, i: (0, 0)),
            pl.BlockSpec((ts, w_b), lambda b, i: (b * n_s + i, gate_col)),
        ],
        out_specs=pl.BlockSpec((ts, w_b), lambda b, i: (b * n_s + i, 0)),
        out_shape=jax.ShapeDtypeStruct((batch * seq, w_b), BF16),
        scratch_shapes=[pltpu.VMEM((w_b // V7X_LANES, CONV_HALO + ts, V7X_LANES), F32),
                        pltpu.VMEM((ts, w_b), F32)],
        compiler_params=_cparams("parallel", "arbitrary"),
        name="conv_module",
    )(pa, pa, conv_w.astype(F32), conv_b.reshape(1, w_b).astype(F32), ln_g.reshape(1, w_b).astype(F32),
      ln_b.reshape(1, w_b).astype(F32), pa)


def _sgu_kernel(ug_ref, v_ref, lg_ref, lb_ref, ws_ref, bs_ref, o_ref):
    tb, width = v_ref.shape
    gw = width // N_GROUPS_C
    pos_r = lax.broadcasted_iota(jnp.int32, (GMLP_CHUNK, GMLP_CHUNK), 0) // CHUNK
    pos_c = lax.broadcasted_iota(jnp.int32, (GMLP_CHUNK, GMLP_CHUNK), 1) // CHUNK
    causal = pos_r >= pos_c
    for n in range(tb // GMLP_CHUNK):
        rows = slice(n * GMLP_CHUNK, (n + 1) * GMLP_CHUNK)
        v = v_ref[rows, :].astype(F32)
        mu = jnp.mean(v, axis=-1, keepdims=True)
        d = v - mu
        var = jnp.mean(d * d, axis=-1, keepdims=True)
        vn = ((d * lax.rsqrt(var + EPS)) * lg_ref[...] + lb_ref[...]).astype(BF16)
        for g in range(N_GROUPS_C):
            cols = slice(g * gw, (g + 1) * gw)
            ws = jnp.where(causal, ws_ref[g], 0.0).astype(BF16)
            sg = _dot(ws, vn[:, cols]) + bs_ref[g]
            o_ref[rows, cols] = (ug_ref[rows, cols].astype(F32) * sg).astype(o_ref.dtype)


def _sgu(uv, ln_g, ln_b, w_s, b_s, *, width, tb):
    m = uv.shape[0]
    assert m % tb == 0 and tb % GMLP_CHUNK == 0
    return pl.pallas_call(
        _sgu_kernel,
        grid=(m // tb,),
        in_specs=[
            pl.BlockSpec((tb, width), lambda i: (i, 0)),
            pl.BlockSpec((tb, width), lambda i: (i, 1)),
            pl.BlockSpec((1, width), lambda i: (0, 0)),
            pl.BlockSpec((1, width), lambda i: (0, 0)),
            pl.BlockSpec((N_GROUPS_C, GMLP_CHUNK, GMLP_CHUNK), lambda i: (0, 0, 0)),
            pl.BlockSpec((N_GROUPS_C, GMLP_CHUNK, 1), lambda i: (0, 0, 0)),
        ],
        out_specs=pl.BlockSpec((tb, width), lambda i: (i, 0)),
        out_shape=jax.ShapeDtypeStruct((m, width), BF16),
        compiler_params=_cparams("parallel"),
        name="sgu",
    )(uv, uv, ln_g.reshape(1, width).astype(F32), ln_b.reshape(1, width).astype(F32),
      w_s.astype(F32), b_s.astype(F32)[..., None])


def _xattn_kernel(q_ref, k_ref, v_ref, wo_ref, r_ref, fg_ref, o_ref, ctx_ref, *, final_norm):
    d = q_ref.shape[1]
    dh = d // N_HEADS_X
    for h in range(N_HEADS_X):
        cols = slice(h * dh, (h + 1) * dh)
        s = _dot_nt(q_ref[:, cols], k_ref[:, cols])
        p = jnp.exp(s - jnp.max(s, axis=-1, keepdims=True))
        denom = jnp.sum(p, axis=-1, keepdims=True)
        ctx_ref[:, cols] = (_dot(p.astype(BF16), v_ref[:, cols]) / denom).astype(BF16)
    h_new = r_ref[...] + _dot(ctx_ref[...], wo_ref[...])
    if final_norm:
        ms = jnp.mean(h_new * h_new, axis=-1, keepdims=True)
        h_new = (h_new * lax.rsqrt(ms + EPS)) * fg_ref[...]
    o_ref[...] = h_new


def _cross_attention(qx, kv, wo, resid, final_g, *, batch, seq, n_mem, tq, final_norm):
    m, d = resid.shape
    assert seq % tq == 0
    n_s = seq // tq
    kern = functools.partial(_xattn_kernel, final_norm=final_norm)
    return pl.pallas_call(
        kern,
        grid=(batch, n_s),
        in_specs=[
            pl.BlockSpec((tq, d), lambda b, i: (b * n_s + i, 0)),
            pl.BlockSpec((n_mem, d), lambda b, i: (b, 0)),
            pl.BlockSpec((n_mem, d), lambda b, i: (b, 1)),
            pl.BlockSpec((d, d), lambda b, i: (0, 0)),
            pl.BlockSpec((tq, d), lambda b, i: (b * n_s + i, 0)),
            pl.BlockSpec((1, d), lambda b, i: (0, 0)),
        ],
        out_specs=pl.BlockSpec((tq, d), lambda b, i: (b * n_s + i, 0)),
        out_shape=jax.ShapeDtypeStruct((m, d), F32),
        scratch_shapes=[pltpu.VMEM((tq, d), BF16)],
        compiler_params=_cparams("parallel", "arbitrary"),
        name="cross_attention",
    )(qx, kv, kv, wo, resid, final_g.reshape(1, d).astype(F32))


def _pick(n, *cands):
    for c in cands:
        if n % c == 0:
            return c
    raise ValueError(f"no tile for {n} among {cands}")


def kernel(x, mem, norm_mix_g, norm_x_g, norm_mem_g, final_norm_g, w_in_ab, rel_bias, conv_w, conv_b,
           conv_ln_g, conv_ln_b, w_out_ab, w_in_c, sgu_ln_g, sgu_ln_b, w_s, b_s, w_out_c, w_xq, w_xk,
           w_xv, w_xo):
    batch, seq, d = x.shape
    n_mem = mem.shape[1]
    depth = norm_mix_g.shape[0]
    mix = w_out_ab.shape[1]
    w_a = mix // 2
    w_b = mix - w_a
    t = batch * seq
    tm = _pick(t, 1024, 512, 256)
    tn = _pick(w_a, 512, 256)
    nb_a, nb_b, nb_mix, nb_d = w_a // tn, w_b // tn, mix // tn, d // tn
    scale_a = HEAD_DIM_A ** -0.5
    scale_x = (d // N_HEADS_X) ** -0.5

    h = x.reshape(t, d).astype(F32)
    mem2 = mem.reshape(batch * n_mem, d).astype(F32)
    for layer in range(depth):
        i = layer // 2
        if layer % 2 == 0:
            segs = [
                (nb_a, "plain", scale_a, 0, None),
                (2 * nb_a, "plain", 1.0, nb_a, None),
                (nb_b, "glu", 1.0, 3 * nb_a, 3 * nb_a + nb_b),
                (nb_mix, "silu", 1.0, 3 * nb_a + 2 * nb_b, None),
            ]
            pa = _norm_matmul(h, norm_mix_g[layer], w_in_ab[i].astype(BF16), segs, tm=tm, tn=tn)
            gate_col = 3 * w_a + w_b
            bias = _bias_tables(rel_bias[i])
            ya = _band_attention(pa, bias, batch=batch, seq=seq, w_a=w_a, gate_col=gate_col)
            yb = _conv_module(pa, conv_w[i], conv_b[i], conv_ln_g[i], conv_ln_b[i], batch=batch, seq=seq,
                              w_b=w_b, x_col=3 * w_a // w_b, gate_col=(gate_col + w_a) // w_b,
                              ts=_pick(seq, 256, 128))
            h = _resid_matmul(ya, 0, yb, 0, w_out_ab[i].astype(BF16), h, tm=tm, tn=tn)
        else:
            segs = [
                (nb_mix, "mulsilu", 1.0, 0, 2 * nb_mix),
                (nb_mix, "plain", 1.0, nb_mix, None),
            ]
            uv = _norm_matmul(h, norm_mix_g[layer], w_in_c[i].astype(BF16), segs, tm=tm, tn=tn)
            y = _sgu(uv, sgu_ln_g[i], sgu_ln_b[i], w_s[i], b_s[i], width=mix, tb=_pick(t, 256, 128))
            h = _resid_matmul(y, 0, y, 1, w_out_c[i].astype(BF16), h, tm=tm, tn=tn)

        w_kv = jnp.concatenate([w_xk[layer], w_xv[layer]], axis=1).astype(BF16)
        kv = _norm_matmul(mem2, norm_mem_g[layer], w_kv, [(2 * nb_d, "plain", 1.0, 0, None)],
                          tm=_pick(batch * n_mem, 512, 256, 128), tn=tn)
        qx = _norm_matmul(h, norm_x_g[layer], w_xq[layer].astype(BF16), [(nb_d, "plain", scale_x, 0, None)],
                          tm=tm, tn=tn)
        h = _cross_attention(qx, kv, w_xo[layer].astype(BF16), h, final_norm_g, batch=batch, seq=seq,
                             n_mem=n_mem, tq=_pick(seq, 256, 128), final_norm=(layer == depth - 1))
    return h.reshape(batch, seq, d)
```

```python
import functools

import numpy as np
import jax
import jax.numpy as jnp
from jax import lax
from jax.experimental import pallas as pl
from jax.experimental.pallas import tpu as pltpu

F32 = jnp.float32
BF16 = jnp.bfloat16

CHUNK = 64
N_PAST_CHUNKS = 8
MAX_REL = 128
HEAD_DIM_A = 128
CONV_WIDTH = 31
GMLP_CHUNK = 128
N_GROUPS_C = 8
N_HEADS_X = 4
EPS = 1e-6
NEG_INF = -1e30

V7X_LANES = 128
V7X_VMEM_LIMIT_BYTES = 56 * 1024 * 1024

ATT_QB = 2 * CHUNK
ATT_WIN = N_PAST_CHUNKS * CHUNK + ATT_QB
ATT_ROLL_WIDTH = 768
ATT_N_VARIANTS = N_PAST_CHUNKS * CHUNK // ATT_QB + 1
REL_ROWS = 384
CONV_HALO = 32


def _cparams(*sem):
    return pltpu.CompilerParams(dimension_semantics=sem, vmem_limit_bytes=V7X_VMEM_LIMIT_BYTES)


def _sigmoid(x):
    return 1.0 / (1.0 + jnp.exp(-x))


def _dot(a, b):
    return jnp.dot(a, b, preferred_element_type=F32)


def _dot_nt(a, b):
    return lax.dot_general(a, b, (((1,), (1,)), ((), ())), preferred_element_type=F32)


def _norm_matmul_kernel(a_tab, b_tab, x_ref, g_ref, wa_ref, wb_ref, o_ref, hn_ref, *, segments):
    del a_tab, b_tab
    j = pl.program_id(1)
    rows_per_iter = 16

    @pl.when(j == 0)
    def _():
        g = g_ref[...]

        def body(r, carry):
            rows = pl.ds(pl.multiple_of(r * rows_per_iter, rows_per_iter), rows_per_iter)
            x = x_ref[rows, :].astype(F32)
            ms = jnp.mean(x * x, axis=-1, keepdims=True)
            hn_ref[rows, :] = ((x * lax.rsqrt(ms + EPS)) * g).astype(BF16)
            return carry

        lax.fori_loop(0, x_ref.shape[0] // rows_per_iter, body, 0, unroll=4)

    lo = 0
    for n_steps, kind, scale in segments:
        hi = lo + n_steps

        @pl.when((j >= lo) & (j < hi))
        def _(kind=kind, scale=scale):
            a = _dot(hn_ref[...], wa_ref[...])
            if kind == "plain":
                r = a if scale == 1.0 else a * scale
            elif kind == "silu":
                r = a * _sigmoid(a)
            elif kind == "glu":
                r = a * _sigmoid(_dot(hn_ref[...], wb_ref[...]))
            elif kind == "mulsilu":
                b = _dot(hn_ref[...], wb_ref[...])
                r = a * (b * _sigmoid(b))
            else:
                raise ValueError(kind)
            o_ref[...] = r.astype(o_ref.dtype)

        lo = hi


def _norm_matmul(x, g, w, segments, *, tm, tn):
    m, k = x.shape
    assert m % tm == 0 and w.shape[0] == k and w.shape[1] % tn == 0
    a_tab, b_tab = [], []
    for n_steps, _, _, a0, b0 in segments:
        for s in range(n_steps):
            a_tab.append(a0 + s)
            b_tab.append(None if b0 is None else b0 + s)
    known = [b for b in b_tab if b is not None]
    fill = known[0] if known else 0
    for idx, b in enumerate(b_tab):
        if b is None:
            b_tab[idx] = fill
        else:
            fill = b
    n_steps_total = len(a_tab)
    kern = functools.partial(_norm_matmul_kernel, segments=tuple(s[:3] for s in segments))
    grid_spec = pltpu.PrefetchScalarGridSpec(
        num_scalar_prefetch=2,
        grid=(m // tm, n_steps_total),
        in_specs=[
            pl.BlockSpec((tm, k), lambda i, j, at, bt: (i, 0)),
            pl.BlockSpec((1, k), lambda i, j, at, bt: (0, 0)),
            pl.BlockSpec((k, tn), lambda i, j, at, bt: (0, at[j])),
            pl.BlockSpec((k, tn), lambda i, j, at, bt: (0, bt[j])),
        ],
        out_specs=pl.BlockSpec((tm, tn), lambda i, j, at, bt: (i, j)),
        scratch_shapes=[pltpu.VMEM((tm, k), BF16)],
    )
    return pl.pallas_call(
        kern,
        grid_spec=grid_spec,
        out_shape=jax.ShapeDtypeStruct((m, n_steps_total * tn), BF16),
        compiler_params=_cparams("parallel", "arbitrary"),
        name="norm_matmul",
    )(jnp.asarray(a_tab, jnp.int32), jnp.asarray(b_tab, jnp.int32), x, g.reshape(1, k), w, w)


def _resid_matmul_kernel(l0_ref, l1_ref, w0_ref, w1_ref, r_ref, o_ref):
    acc = _dot(l0_ref[...], w0_ref[...]) + _dot(l1_ref[...], w1_ref[...])
    o_ref[...] = r_ref[...] + acc


def _resid_matmul(l0, c0, l1, c1, w, resid, *, tm, tn):
    m, n = resid.shape
    kh = w.shape[0] // 2
    assert m % tm == 0 and n % tn == 0
    return pl.pallas_call(
        _resid_matmul_kernel,
        grid=(m // tm, n // tn),
        in_specs=[
            pl.BlockSpec((tm, kh), lambda i, j: (i, c0)),
            pl.BlockSpec((tm, kh), lambda i, j: (i, c1)),
            pl.BlockSpec((kh, tn), lambda i, j: (0, j)),
            pl.BlockSpec((kh, tn), lambda i, j: (1, j)),
            pl.BlockSpec((tm, tn), lambda i, j: (i, j)),
        ],
        out_specs=pl.BlockSpec((tm, tn), lambda i, j: (i, j)),
        out_shape=jax.ShapeDtypeStruct((m, n), F32),
        compiler_params=_cparams("parallel", "arbitrary"),
        name="resid_matmul",
    )(l0, l1, w, w, resid)


def _bias_table_kernel(rb_ref, o_ref, *, n_heads):
    t = pl.program_id(0)
    width = ATT_ROLL_WIDTH
    m = lax.broadcasted_iota(jnp.int32, (REL_ROWS, width), 1)
    r = lax.broadcasted_iota(jnp.int32, (REL_ROWS, width), 0)
    mm = jnp.where(m < ATT_WIN, m, m - width)
    idx = jnp.clip(ATT_QB * t - mm, -MAX_REL, MAX_REL) + MAX_REL
    onehot = jnp.where(r == idx, 1.0, 0.0).astype(BF16)
    rb = rb_ref[...]
    hi = rb.astype(BF16)
    rem = rb - hi.astype(F32)
    mid = rem.astype(BF16)
    lo = (rem - mid.astype(F32)).astype(BF16)
    u = _dot(hi, onehot) + _dot(mid, onehot) + _dot(lo, onehot)

    qi = lax.broadcasted_iota(jnp.int32, (ATT_QB, ATT_WIN), 0) // CHUNK
    kj = lax.broadcasted_iota(jnp.int32, (ATT_QB, ATT_WIN), 1) // CHUNK
    q_chunk = (ATT_QB // CHUNK) * t + qi
    valid = (kj <= q_chunk) & (kj >= q_chunk - N_PAST_CHUNKS)
    for h in range(n_heads):
        x = jnp.broadcast_to(u[h:h + 1, :], (ATT_QB, width))
        toeplitz = pltpu.roll(x, 0, 1, stride=1, stride_axis=0)
        o_ref[0, h] = jnp.where(valid, toeplitz[:, :ATT_WIN], NEG_INF)


def _bias_tables(rel_bias):
    n_heads, n_rel = rel_bias.shape
    rb = jnp.pad(rel_bias.astype(F32), ((0, 0), (0, REL_ROWS - n_rel)))
    return pl.pallas_call(
        functools.partial(_bias_table_kernel, n_heads=n_heads),
        grid=(ATT_N_VARIANTS,),
        in_specs=[pl.BlockSpec((n_heads, REL_ROWS), lambda t: (0, 0))],
        out_specs=pl.BlockSpec((1, n_heads, ATT_QB, ATT_WIN), lambda t: (t, 0, 0, 0)),
        out_shape=jax.ShapeDtypeStruct((ATT_N_VARIANTS, n_heads, ATT_QB, ATT_WIN), F32),
        compiler_params=_cparams("arbitrary"),
        name="bias_tables",
    )(rb)


def _band_attn_kernel(q_ref, k_ref, v_ref, bias_ref, gate_ref, o_ref,
                      s_even, s_odd, p_even, p_odd, l_even, l_odd, *, heads_per_step):
    dh = HEAD_DIM_A
    n_blocks = q_ref.shape[0] // ATT_QB
    last_variant = ATT_N_VARIANTS - 1
    heads = [(h, slice(h * dh, (h + 1) * dh)) for h in range(heads_per_step)]

    def window(blk):
        return pl.ds(pl.multiple_of(jnp.maximum(blk - last_variant, 0) * ATT_QB, ATT_QB), ATT_WIN)

    def block_rows(blk):
        return pl.ds(pl.multiple_of(blk * ATT_QB, ATT_QB), ATT_QB)

    def scores(blk, s_buf):
        variant = jnp.minimum(blk, last_variant)
        for h, cols in heads:
            s_buf[h] = _dot_nt(q_ref[block_rows(blk), cols], k_ref[window(blk), cols]) + bias_ref[variant, h]

    def exponentiate(s_buf, p_buf, l_buf):
        for h, _ in heads:
            s = s_buf[h]
            p = jnp.exp(s - jnp.max(s, axis=-1, keepdims=True))
            l_buf[h] = jnp.sum(p, axis=-1, keepdims=True)
            p_buf[h] = p.astype(BF16)

    def apply_values(blk, p_buf, l_buf):
        rows = block_rows(blk)
        for h, cols in heads:
            o = _dot(p_buf[h], v_ref[window(blk), cols]) / l_buf[h]
            o_ref[rows, cols] = (o * gate_ref[rows, cols].astype(F32)).astype(o_ref.dtype)

    scores(0, s_even)
    scores(1, s_odd)
    exponentiate(s_even, p_even, l_even)

    def pair(t, carry):
        m = 2 * t + 1
        scores(m + 1, s_even)
        exponentiate(s_odd, p_odd, l_odd)
        apply_values(m - 1, p_even, l_even)
        scores(m + 2, s_odd)
        exponentiate(s_even, p_even, l_even)
        apply_values(m, p_odd, l_odd)
        return carry

    lax.fori_loop(0, (n_blocks - 2) // 2, pair, 0)
    exponentiate(s_odd, p_odd, l_odd)
    apply_values(n_blocks - 2, p_even, l_even)
    apply_values(n_blocks - 1, p_odd, l_odd)


def _band_attention(pa, bias, *, batch, seq, w_a, gate_col):
    heads_per_step = 2
    cw = heads_per_step * HEAD_DIM_A
    n_blocks = seq // ATT_QB
    assert seq % ATT_QB == 0 and n_blocks % 2 == 0 and n_blocks >= 4
    assert w_a % cw == 0 and gate_col % cw == 0
    kern = functools.partial(_band_attn_kernel, heads_per_step=heads_per_step)
    seq_cols = lambda first: pl.BlockSpec((seq, cw), lambda b, g: (b, first // cw + g))
    return pl.pallas_call(
        kern,
        grid=(batch, w_a // cw),
        in_specs=[
            seq_cols(0),
            seq_cols(w_a),
            seq_cols(2 * w_a),
            pl.BlockSpec((ATT_N_VARIANTS, heads_per_step, ATT_QB, ATT_WIN), lambda b, g: (0, g, 0, 0)),
            seq_cols(gate_col),
        ],
        out_specs=seq_cols(0),
        out_shape=jax.ShapeDtypeStruct((batch * seq, w_a), BF16),
        scratch_shapes=[
            pltpu.VMEM((heads_per_step, ATT_QB, ATT_WIN), F32),
            pltpu.VMEM((heads_per_step, ATT_QB, ATT_WIN), F32),
            pltpu.VMEM((heads_per_step, ATT_QB, ATT_WIN), BF16),
            pltpu.VMEM((heads_per_step, ATT_QB, ATT_WIN), BF16),
            pltpu.VMEM((heads_per_step, ATT_QB, 1), F32),
            pltpu.VMEM((heads_per_step, ATT_QB, 1), F32),
        ],
        compiler_params=_cparams("parallel", "parallel"),
        name="band_attention",
    )(pa, pa, pa, bias, pa)


def _conv_module_kernel(x_ref, halo_ref, w_ref, cb_ref, lg_ref, lb_ref, gate_ref, o_ref, xw_ref, y_ref):
    ts, c = x_ref.shape
    i = pl.program_id(1)
    n_lane_blocks = c // V7X_LANES
    for lb in range(n_lane_blocks):
        cols = slice(lb * V7X_LANES, (lb + 1) * V7X_LANES)
        xw_ref[lb, CONV_HALO:CONV_HALO + ts, :] = x_ref[:, cols].astype(F32)

    @pl.when(i == 0)
    def _():
        xw_ref[:, 0:CONV_HALO, :] = jnp.zeros((n_lane_blocks, CONV_HALO, V7X_LANES), F32)

    @pl.when(i > 0)
    def _():
        for lb in range(n_lane_blocks):
            cols = slice(lb * V7X_LANES, (lb + 1) * V7X_LANES)
            xw_ref[lb, 0:CONV_HALO, :] = halo_ref[:, cols].astype(F32)

    row_blk = 128
    sub = 8
    first_tap = CONV_HALO - (CONV_WIDTH - 1)

    def conv_body(it, carry):
        rb = it // n_lane_blocks
        lb = it % n_lane_blocks
        cols = pl.ds(pl.multiple_of(lb * V7X_LANES, V7X_LANES), V7X_LANES)
        base = rb * row_blk
        acc = [jnp.broadcast_to(cb_ref[:, cols], (sub, V7X_LANES))] * (row_blk // sub)
        for tap in range(CONV_WIDTH):
            wk = jnp.broadcast_to(w_ref[tap:tap + 1, cols], (sub, V7X_LANES))
            acc = [a + wk * xw_ref[lb, pl.ds(base + r * sub + first_tap + tap, sub), :]
                   for r, a in enumerate(acc)]
        for r, a in enumerate(acc):
            y_ref[pl.ds(pl.multiple_of(base + r * sub, sub), sub), cols] = a
        return carry

    lax.fori_loop(0, (ts // row_blk) * n_lane_blocks, conv_body, 0)

    ln_rows = 16

    def ln_body(r, carry):
        rows = pl.ds(pl.multiple_of(r * ln_rows, ln_rows), ln_rows)
        y = y_ref[rows, :]
        mu = jnp.mean(y, axis=-1, keepdims=True)
        d = y - mu
        var = jnp.mean(d * d, axis=-1, keepdims=True)
        z = (d * lax.rsqrt(var + EPS)) * lg_ref[...] + lb_ref[...]
        z = z * _sigmoid(z)
        o_ref[rows, :] = (z * gate_ref[rows, :].astype(F32)).astype(o_ref.dtype)
        return carry

    lax.fori_loop(0, ts // ln_rows, ln_body, 0, unroll=4)


def _conv_module(pa, conv_w, conv_b, ln_g, ln_b, *, batch, seq, w_b, x_col, gate_col, ts):
    assert seq % ts == 0 and ts % 128 == 0 and ts % CONV_HALO == 0
    n_s = seq // ts
    halo_per_blk = ts // CONV_HALO
    return pl.pallas_call(
        _conv_module_kernel,
        grid=(batch, n_s),
        in_specs=[
            pl.BlockSpec((ts, w_b), lambda b, i: (b * n_s + i, x_col)),
            pl.BlockSpec((CONV_HALO, w_b),
                         lambda b, i: (jnp.maximum((b * n_s + i) * halo_per_blk - 1, 0), x_col)),
            pl.BlockSpec((CONV_WIDTH, w_b), lambda b, i: (0, 0)),
            pl.BlockSpec((1, w_b), lambda b, i: (0, 0)),
            pl.BlockSpec((1, w_b), lambda b, i: (0, 0)),
            pl.BlockSpec((1, w_b), lambda b, i: (0, 0)),
            pl.BlockSpec((ts, w_b), lambda b, i: (b * n_s + i, gate_col)),
        ],
        out_specs=pl.BlockSpec((ts, w_b), lambda b, i: (b * n_s + i, 0)),
        out_shape=jax.ShapeDtypeStruct((batch * seq, w_b), BF16),
        scratch_shapes=[pltpu.VMEM((w_b // V7X_LANES, CONV_HALO + ts, V7X_LANES), F32),
                        pltpu.VMEM((ts, w_b), F32)],
        compiler_params=_cparams("parallel", "arbitrary"),
        name="conv_module",
    )(pa, pa, conv_w.astype(F32), conv_b.reshape(1, w_b).astype(F32), ln_g.reshape(1, w_b).astype(F32),
      ln_b.reshape(1, w_b).astype(F32), pa)


def _sgu_kernel(ug_ref, v_ref, lg_ref, lb_ref, ws_ref, bs_ref, o_ref):
    tb, width = v_ref.shape
    gw = width // N_GROUPS_C
    pos_r = lax.broadcasted_iota(jnp.int32, (GMLP_CHUNK, GMLP_CHUNK), 0) // CHUNK
    pos_c = lax.broadcasted_iota(jnp.int32, (GMLP_CHUNK, GMLP_CHUNK), 1) // CHUNK
    causal = pos_r >= pos_c
    for n in range(tb // GMLP_CHUNK):
        rows = slice(n * GMLP_CHUNK, (n + 1) * GMLP_CHUNK)
        v = v_ref[rows, :].astype(F32)
        mu = jnp.mean(v, axis=-1, keepdims=True)
        d = v - mu
        var = jnp.mean(d * d, axis=-1, keepdims=True)
        vn = ((d * lax.rsqrt(var + EPS)) * lg_ref[...] + lb_ref[...]).astype(BF16)
        for g in range(N_GROUPS_C):
            cols = slice(g * gw, (g + 1) * gw)
            ws = jnp.where(causal, ws_ref[g], 0.0).astype(BF16)
            sg = _dot(ws, vn[:, cols]) + bs_ref[g]
            o_ref[rows, cols] = (ug_ref[rows, cols].astype(F32) * sg).astype(o_ref.dtype)


def _sgu(uv, ln_g, ln_b, w_s, b_s, *, width, tb):
    m = uv.shape[0]
    assert m % tb == 0 and tb % GMLP_CHUNK == 0
    return pl.pallas_call(
        _sgu_kernel,
        grid=(m // tb,),
        in_specs=[
            pl.BlockSpec((tb, width), lambda i: (i, 0)),
            pl.BlockSpec((tb, width), lambda i: (i, 1)),
            pl.BlockSpec((1, width), lambda i: (0, 0)),
            pl.BlockSpec((1, width), lambda i: (0, 0)),
            pl.BlockSpec((N_GROUPS_C, GMLP_CHUNK, GMLP_CHUNK), lambda i: (0, 0, 0)),
            pl.BlockSpec((N_GROUPS_C, GMLP_CHUNK, 1), lambda i: (0, 0, 0)),
        ],
        out_specs=pl.BlockSpec((tb, width), lambda i: (i, 0)),
        out_shape=jax.ShapeDtypeStruct((m, width), BF16),
        compiler_params=_cparams("parallel"),
        name="sgu",
    )(uv, uv, ln_g.reshape(1, width).astype(F32), ln_b.reshape(1, width).astype(F32),
      w_s.astype(F32), b_s.astype(F32)[..., None])


def _xattn_kernel(h_ref, g_ref, wq_ref, k_ref, v_ref, wo_ref, fg_ref, o_ref, hn_ref, q_ref, ctx_ref,
                  *, scale, final_norm):
    d = h_ref.shape[1]
    dh = d // N_HEADS_X
    x = h_ref[...]
    ms = jnp.mean(x * x, axis=-1, keepdims=True)
    hn_ref[...] = ((x * lax.rsqrt(ms + EPS)) * g_ref[...]).astype(BF16)
    q_ref[...] = (_dot(hn_ref[...], wq_ref[...]) * scale).astype(BF16)
    for h in range(N_HEADS_X):
        cols = slice(h * dh, (h + 1) * dh)
        s = _dot_nt(q_ref[:, cols], k_ref[:, cols])
        p = jnp.exp(s - jnp.max(s, axis=-1, keepdims=True))
        denom = jnp.sum(p, axis=-1, keepdims=True)
        ctx_ref[:, cols] = (_dot(p.astype(BF16), v_ref[:, cols]) / denom).astype(BF16)
    h_new = h_ref[...] + _dot(ctx_ref[...], wo_ref[...])
    if final_norm:
        ms = jnp.mean(h_new * h_new, axis=-1, keepdims=True)
        h_new = (h_new * lax.rsqrt(ms + EPS)) * fg_ref[...]
    o_ref[...] = h_new


def _cross_attention(h, g, wq, kv, wo, final_g, *, batch, seq, n_mem, tq, final_norm):
    m, d = h.shape
    assert seq % tq == 0
    n_s = seq // tq
    kern = functools.partial(_xattn_kernel, scale=(d // N_HEADS_X) ** -0.5, final_norm=final_norm)
    resident = dict(pipeline_mode=pl.Buffered(1))
    return pl.pallas_call(
        kern,
        grid=(batch, n_s),
        in_specs=[
            pl.BlockSpec((tq, d), lambda b, i: (b * n_s + i, 0)),
            pl.BlockSpec((1, d), lambda b, i: (0, 0)),
            pl.BlockSpec((d, d), lambda b, i: (0, 0), **resident),
            pl.BlockSpec((n_mem, d), lambda b, i: (b, 0)),
            pl.BlockSpec((n_mem, d), lambda b, i: (b, 1)),
            pl.BlockSpec((d, d), lambda b, i: (0, 0), **resident),
            pl.BlockSpec((1, d), lambda b, i: (0, 0)),
        ],
        out_specs=pl.BlockSpec((tq, d), lambda b, i: (b * n_s + i, 0)),
        out_shape=jax.ShapeDtypeStruct((m, d), F32),
        scratch_shapes=[pltpu.VMEM((tq, d), BF16), pltpu.VMEM((tq, d), BF16), pltpu.VMEM((tq, d), BF16)],
        compiler_params=_cparams("parallel", "arbitrary"),
        name="cross_attention",
    )(h, g.reshape(1, d).astype(F32), wq, kv, kv, wo, final_g.reshape(1, d).astype(F32))


def _pick(n, *cands):
    for c in cands:
        if n % c == 0:
            return c
    raise ValueError(f"no tile for {n} among {cands}")


def kernel(x, mem, norm_mix_g, norm_x_g, norm_mem_g, final_norm_g, w_in_ab, rel_bias, conv_w, conv_b,
           conv_ln_g, conv_ln_b, w_out_ab, w_in_c, sgu_ln_g, sgu_ln_b, w_s, b_s, w_out_c, w_xq, w_xk,
           w_xv, w_xo):
    batch, seq, d = x.shape
    n_mem = mem.shape[1]
    depth = norm_mix_g.shape[0]
    mix = w_out_ab.shape[1]
    w_a = mix // 2
    w_b = mix - w_a
    t = batch * seq
    tm = _pick(t, 1024, 512, 256)
    tn = _pick(w_a, 1024, 512, 256)
    nb_a, nb_b, nb_mix, nb_d = w_a // tn, w_b // tn, mix // tn, d // tn
    scale_a = HEAD_DIM_A ** -0.5
    scale_x = (d // N_HEADS_X) ** -0.5

    h = x.reshape(t, d).astype(F32)
    mem2 = mem.reshape(batch * n_mem, d).astype(F32)
    for layer in range(depth):
        i = layer // 2
        if layer % 2 == 0:
            segs = [
                (nb_a, "plain", scale_a, 0, None),
                (2 * nb_a, "plain", 1.0, nb_a, None),
                (nb_b, "glu", 1.0, 3 * nb_a, 3 * nb_a + nb_b),
                (nb_mix, "silu", 1.0, 3 * nb_a + 2 * nb_b, None),
            ]
            pa = _norm_matmul(h, norm_mix_g[layer], w_in_ab[i].astype(BF16), segs, tm=tm, tn=tn)
            gate_col = 3 * w_a + w_b
            bias = _bias_tables(rel_bias[i])
            ya = _band_attention(pa, bias, batch=batch, seq=seq, w_a=w_a, gate_col=gate_col)
            yb = _conv_module(pa, conv_w[i], conv_b[i], conv_ln_g[i], conv_ln_b[i], batch=batch, seq=seq,
                              w_b=w_b, x_col=3 * w_a // w_b, gate_col=(gate_col + w_a) // w_b,
                              ts=_pick(seq, 256, 128))
            h = _resid_matmul(ya, 0, yb, 0, w_out_ab[i].astype(BF16), h, tm=tm, tn=tn)
        else:
            segs = [
                (nb_mix, "mulsilu", 1.0, 0, 2 * nb_mix),
                (nb_mix, "plain", 1.0, nb_mix, None),
            ]
            uv = _norm_matmul(h, norm_mix_g[layer], w_in_c[i].astype(BF16), segs, tm=tm, tn=tn)
            y = _sgu(uv, sgu_ln_g[i], sgu_ln_b[i], w_s[i], b_s[i], width=mix, tb=_pick(t, 256, 128))
            h = _resid_matmul(y, 0, y, 1, w_out_c[i].astype(BF16), h, tm=tm, tn=tn)

        w_kv = jnp.concatenate([w_xk[layer], w_xv[layer]], axis=1).astype(BF16)
        kv = _norm_matmul(mem2, norm_mem_g[layer], w_kv, [(2 * nb_d, "plain", 1.0, 0, None)],
                          tm=_pick(batch * n_mem, 512, 256, 128), tn=tn)
        h = _cross_attention(h, norm_x_g[layer], w_xq[layer].astype(BF16), kv, w_xo[layer].astype(BF16),
                             final_norm_g, batch=batch, seq=seq, n_mem=n_mem, tq=_pick(seq, 512, 256, 128),
                             final_norm=(layer == depth - 1))
    return h.reshape(batch, seq, d)
```

```python
import functools

import jax
import jax.numpy as jnp
from jax import lax
from jax.experimental import pallas as pl
from jax.experimental.pallas import tpu as pltpu

F32 = jnp.float32
BF16 = jnp.bfloat16

CHUNK = 64
N_PAST_CHUNKS = 8
MAX_REL = 128
HEAD_DIM_A = 128
CONV_WIDTH = 31
GMLP_CHUNK = 128
N_GROUPS_C = 8
N_HEADS_X = 4
EPS = 1e-6
NEG_INF = -1e30

V7X_LANES = 128
V7X_VMEM_LIMIT_BYTES = 56 * 1024 * 1024

ATT_QB = 2 * CHUNK
ATT_WIN = N_PAST_CHUNKS * CHUNK + ATT_QB
ATT_ROLL_WIDTH = 768
ATT_N_VARIANTS = N_PAST_CHUNKS * CHUNK // ATT_QB + 1
REL_ROWS = 384
CONV_HALO = 32


def _cparams(*sem):
    return pltpu.CompilerParams(dimension_semantics=sem, vmem_limit_bytes=V7X_VMEM_LIMIT_BYTES)


def _sigmoid(x):
    return 0.5 * jnp.tanh(0.5 * x) + 0.5


def _dot(a, b):
    return jnp.dot(a, b, preferred_element_type=F32)


def _dot_nt(a, b):
    return lax.dot_general(a, b, (((1,), (1,)), ((), ())), preferred_element_type=F32)


def _norm_matmul_kernel(a_tab, b_tab, x_ref, g_ref, wa_ref, wb_ref, o_ref, hn_ref, *, segments):
    del a_tab, b_tab
    j = pl.program_id(1)
    rows_per_iter = 16

    @pl.when(j == 0)
    def _():
        g = g_ref[...]

        def body(r, carry):
            rows = pl.ds(pl.multiple_of(r * rows_per_iter, rows_per_iter), rows_per_iter)
            x = x_ref[rows, :].astype(F32)
            ms = jnp.mean(x * x, axis=-1, keepdims=True)
            hn_ref[rows, :] = ((x * lax.rsqrt(ms + EPS)) * g).astype(BF16)
            return carry

        lax.fori_loop(0, x_ref.shape[0] // rows_per_iter, body, 0, unroll=4)

    lo = 0
    for n_steps, kind, scale in segments:
        hi = lo + n_steps

        @pl.when((j >= lo) & (j < hi))
        def _(kind=kind, scale=scale):
            a = _dot(hn_ref[...], wa_ref[...])
            if kind == "plain":
                r = a if scale == 1.0 else a * scale
            elif kind == "silu":
                r = a * _sigmoid(a)
            elif kind == "glu":
                r = a * _sigmoid(_dot(hn_ref[...], wb_ref[...]))
            elif kind == "mulsilu":
                b = _dot(hn_ref[...], wb_ref[...])
                r = a * (b * _sigmoid(b))
            else:
                raise ValueError(kind)
            o_ref[...] = r.astype(o_ref.dtype)

        lo = hi


def _norm_matmul(x, g, w, segments, *, tm, tn):
    m, k = x.shape
    assert m % tm == 0 and w.shape[0] == k and w.shape[1] % tn == 0
    a_tab, b_tab = [], []
    for n_steps, _, _, a0, b0 in segments:
        for s in range(n_steps):
            a_tab.append(a0 + s)
            b_tab.append(None if b0 is None else b0 + s)
    known = [b for b in b_tab if b is not None]
    fill = known[0] if known else 0
    for idx, b in enumerate(b_tab):
        if b is None:
            b_tab[idx] = fill
        else:
            fill = b
    n_steps_total = len(a_tab)
    kern = functools.partial(_norm_matmul_kernel, segments=tuple(s[:3] for s in segments))
    grid_spec = pltpu.PrefetchScalarGridSpec(
        num_scalar_prefetch=2,
        grid=(m // tm, n_steps_total),
        in_specs=[
            pl.BlockSpec((tm, k), lambda i, j, at, bt: (i, 0)),
            pl.BlockSpec((1, k), lambda i, j, at, bt: (0, 0)),
            pl.BlockSpec((k, tn), lambda i, j, at, bt: (0, at[j])),
            pl.BlockSpec((k, tn), lambda i, j, at, bt: (0, bt[j])),
        ],
        out_specs=pl.BlockSpec((tm, tn), lambda i, j, at, bt: (i, j)),
        scratch_shapes=[pltpu.VMEM((tm, k), BF16)],
    )
    return pl.pallas_call(
        kern,
        grid_spec=grid_spec,
        out_shape=jax.ShapeDtypeStruct((m, n_steps_total * tn), BF16),
        compiler_params=_cparams("parallel", "arbitrary"),
        name="norm_matmul",
    )(jnp.asarray(a_tab, jnp.int32), jnp.asarray(b_tab, jnp.int32), x, g.reshape(1, k), w, w)


def _resid_matmul_kernel(l0_ref, l1_ref, w0_ref, w1_ref, r_ref, o_ref):
    acc = _dot(l0_ref[...], w0_ref[...]) + _dot(l1_ref[...], w1_ref[...])
    o_ref[...] = r_ref[...] + acc


def _resid_matmul(l0, c0, l1, c1, w, resid, *, tm, tn):
    m, n = resid.shape
    kh = w.shape[0] // 2
    assert m % tm == 0 and n % tn == 0
    return pl.pallas_call(
        _resid_matmul_kernel,
        grid=(m // tm, n // tn),
        in_specs=[
            pl.BlockSpec((tm, kh), lambda i, j: (i, c0)),
            pl.BlockSpec((tm, kh), lambda i, j: (i, c1)),
            pl.BlockSpec((kh, tn), lambda i, j: (0, j)),
            pl.BlockSpec((kh, tn), lambda i, j: (1, j)),
            pl.BlockSpec((tm, tn), lambda i, j: (i, j)),
        ],
        out_specs=pl.BlockSpec((tm, tn), lambda i, j: (i, j)),
        out_shape=jax.ShapeDtypeStruct((m, n), F32),
        compiler_params=_cparams("parallel", "arbitrary"),
        name="resid_matmul",
    )(l0, l1, w, w, resid)


def _bias_table_kernel(rb_ref, o_ref, *, n_heads):
    t = pl.program_id(0)
    width = ATT_ROLL_WIDTH
    m = lax.broadcasted_iota(jnp.int32, (REL_ROWS, width), 1)
    r = lax.broadcasted_iota(jnp.int32, (REL_ROWS, width), 0)
    mm = jnp.where(m < ATT_WIN, m, m - width)
    idx = jnp.clip(ATT_QB * t - mm, -MAX_REL, MAX_REL) + MAX_REL
    onehot = jnp.where(r == idx, 1.0, 0.0).astype(BF16)
    rb = rb_ref[...]
    hi = rb.astype(BF16)
    rem = rb - hi.astype(F32)
    mid = rem.astype(BF16)
    lo = (rem - mid.astype(F32)).astype(BF16)
    u = _dot(hi, onehot) + _dot(mid, onehot) + _dot(lo, onehot)

    qi = lax.broadcasted_iota(jnp.int32, (ATT_QB, ATT_WIN), 0) // CHUNK
    kj = lax.broadcasted_iota(jnp.int32, (ATT_QB, ATT_WIN), 1) // CHUNK
    q_chunk = (ATT_QB // CHUNK) * t + qi
    valid = (kj <= q_chunk) & (kj >= q_chunk - N_PAST_CHUNKS)
    for h in range(n_heads):
        x = jnp.broadcast_to(u[h:h + 1, :], (ATT_QB, width))
        toeplitz = pltpu.roll(x, 0, 1, stride=1, stride_axis=0)
        o_ref[0, h] = jnp.where(valid, toeplitz[:, :ATT_WIN], NEG_INF)


def _bias_tables(rel_bias):
    n_heads, n_rel = rel_bias.shape
    rb = jnp.pad(rel_bias.astype(F32), ((0, 0), (0, REL_ROWS - n_rel)))
    return pl.pallas_call(
        functools.partial(_bias_table_kernel, n_heads=n_heads),
        grid=(ATT_N_VARIANTS,),
        in_specs=[pl.BlockSpec((n_heads, REL_ROWS), lambda t: (0, 0))],
        out_specs=pl.BlockSpec((1, n_heads, ATT_QB, ATT_WIN), lambda t: (t, 0, 0, 0)),
        out_shape=jax.ShapeDtypeStruct((ATT_N_VARIANTS, n_heads, ATT_QB, ATT_WIN), F32),
        compiler_params=_cparams("arbitrary"),
        name="bias_tables",
    )(rb)


def _band_attn_kernel(q_ref, k_ref, v_ref, bias_ref, gate_ref, o_ref,
                      s_even, s_odd, p_even, p_odd, l_even, l_odd, *, heads_per_step):
    dh = HEAD_DIM_A
    n_blocks = q_ref.shape[0] // ATT_QB
    last_variant = ATT_N_VARIANTS - 1
    heads = [(h, slice(h * dh, (h + 1) * dh)) for h in range(heads_per_step)]

    def window(blk):
        return pl.ds(pl.multiple_of(jnp.maximum(blk - last_variant, 0) * ATT_QB, ATT_QB), ATT_WIN)

    def block_rows(blk):
        return pl.ds(pl.multiple_of(blk * ATT_QB, ATT_QB), ATT_QB)

    def scores(blk, s_buf):
        variant = jnp.minimum(blk, last_variant)
        for h, cols in heads:
            s_buf[h] = _dot_nt(q_ref[block_rows(blk), cols], k_ref[window(blk), cols]) + bias_ref[variant, h]

    def exponentiate(s_buf, p_buf, l_buf):
        for h, _ in heads:
            s = s_buf[h]
            p = jnp.exp(s - jnp.max(s, axis=-1, keepdims=True))
            l_buf[h] = jnp.sum(p, axis=-1, keepdims=True)
            p_buf[h] = p.astype(BF16)

    def apply_values(blk, p_buf, l_buf):
        rows = block_rows(blk)
        for h, cols in heads:
            o = _dot(p_buf[h], v_ref[window(blk), cols]) / l_buf[h]
            o_ref[rows, cols] = (o * gate_ref[rows, cols].astype(F32)).astype(o_ref.dtype)

    scores(0, s_even)
    scores(1, s_odd)
    exponentiate(s_even, p_even, l_even)

    def pair(t, carry):
        m = 2 * t + 1
        scores(m + 1, s_even)
        exponentiate(s_odd, p_odd, l_odd)
        apply_values(m - 1, p_even, l_even)
        scores(m + 2, s_odd)
        exponentiate(s_even, p_even, l_even)
        apply_values(m, p_odd, l_odd)
        return carry

    lax.fori_loop(0, (n_blocks - 2) // 2, pair, 0)
    exponentiate(s_odd, p_odd, l_odd)
    apply_values(n_blocks - 2, p_even, l_even)
    apply_values(n_blocks - 1, p_odd, l_odd)


def _band_attention(pa, bias, *, batch, seq, w_a, gate_col):
    heads_per_step = 2
    cw = heads_per_step * HEAD_DIM_A
    n_blocks = seq // ATT_QB
    assert seq % ATT_QB == 0 and n_blocks % 2 == 0 and n_blocks >= 4
    assert w_a % cw == 0 and gate_col % cw == 0
    kern = functools.partial(_band_attn_kernel, heads_per_step=heads_per_step)
    seq_cols = lambda first: pl.BlockSpec((seq, cw), lambda b, g: (b, first // cw + g))
    return pl.pallas_call(
        kern,
        grid=(batch, w_a // cw),
        in_specs=[
            seq_cols(0),
            seq_cols(w_a),
            seq_cols(2 * w_a),
            pl.BlockSpec((ATT_N_VARIANTS, heads_per_step, ATT_QB, ATT_WIN), lambda b, g: (0, g, 0, 0)),
            seq_cols(gate_col),
        ],
        out_specs=seq_cols(0),
        out_shape=jax.ShapeDtypeStruct((batch * seq, w_a), BF16),
        scratch_shapes=[
            pltpu.VMEM((heads_per_step, ATT_QB, ATT_WIN), F32),
            pltpu.VMEM((heads_per_step, ATT_QB, ATT_WIN), F32),
            pltpu.VMEM((heads_per_step, ATT_QB, ATT_WIN), BF16),
            pltpu.VMEM((heads_per_step, ATT_QB, ATT_WIN), BF16),
            pltpu.VMEM((heads_per_step, ATT_QB, 1), F32),
            pltpu.VMEM((heads_per_step, ATT_QB, 1), F32),
        ],
        compiler_params=_cparams("parallel", "parallel"),
        name="band_attention",
    )(pa, pa, pa, bias, pa)


def _conv_module_kernel(x_ref, halo_ref, w_ref, cb_ref, lg_ref, lb_ref, gate_ref, o_ref, xw_ref, y_ref):
    ts, c = x_ref.shape
    i = pl.program_id(1)
    n_lane_blocks = c // V7X_LANES
    for lb in range(n_lane_blocks):
        cols = slice(lb * V7X_LANES, (lb + 1) * V7X_LANES)
        xw_ref[lb, CONV_HALO:CONV_HALO + ts, :] = x_ref[:, cols].astype(F32)

    @pl.when(i == 0)
    def _():
        xw_ref[:, 0:CONV_HALO, :] = jnp.zeros((n_lane_blocks, CONV_HALO, V7X_LANES), F32)

    @pl.when(i > 0)
    def _():
        for lb in range(n_lane_blocks):
            cols = slice(lb * V7X_LANES, (lb + 1) * V7X_LANES)
            xw_ref[lb, 0:CONV_HALO, :] = halo_ref[:, cols].astype(F32)

    row_blk = 128
    sub = 8
    first_tap = CONV_HALO - (CONV_WIDTH - 1)

    def conv_body(it, carry):
        rb = it // n_lane_blocks
        lb = it % n_lane_blocks
        cols = pl.ds(pl.multiple_of(lb * V7X_LANES, V7X_LANES), V7X_LANES)
        base = rb * row_blk
        acc = [jnp.broadcast_to(cb_ref[:, cols], (sub, V7X_LANES))] * (row_blk // sub)
        for tap in range(CONV_WIDTH):
            wk = jnp.broadcast_to(w_ref[tap:tap + 1, cols], (sub, V7X_LANES))
            acc = [a + wk * xw_ref[lb, pl.ds(base + r * sub + first_tap + tap, sub), :]
                   for r, a in enumerate(acc)]
        for r, a in enumerate(acc):
            y_ref[pl.ds(pl.multiple_of(base + r * sub, sub), sub), cols] = a
        return carry

    lax.fori_loop(0, (ts // row_blk) * n_lane_blocks, conv_body, 0, unroll=2)

    ln_rows = 16

    def ln_body(r, carry):
        rows = pl.ds(pl.multiple_of(r * ln_rows, ln_rows), ln_rows)
        y = y_ref[rows, :]
        mu = jnp.mean(y, axis=-1, keepdims=True)
        d = y - mu
        var = jnp.mean(d * d, axis=-1, keepdims=True)
        z = (d * lax.rsqrt(var + EPS)) * lg_ref[...] + lb_ref[...]
        z = z * _sigmoid(z)
        o_ref[rows, :] = (z * gate_ref[rows, :].astype(F32)).astype(o_ref.dtype)
        return carry

    lax.fori_loop(0, ts // ln_rows, ln_body, 0, unroll=8)


def _conv_module(pa, conv_w, conv_b, ln_g, ln_b, *, batch, seq, w_b, x_col, gate_col, ts):
    assert seq % ts == 0 and ts % 128 == 0 and ts % CONV_HALO == 0
    n_s = seq // ts
    halo_per_blk = ts // CONV_HALO
    return pl.pallas_call(
        _conv_module_kernel,
        grid=(batch, n_s),
        in_specs=[
            pl.BlockSpec((ts, w_b), lambda b, i: (b * n_s + i, x_col)),
            pl.BlockSpec((CONV_HALO, w_b),
                         lambda b, i: (jnp.maximum((b * n_s + i) * halo_per_blk - 1, 0), x_col)),
            pl.BlockSpec((CONV_WIDTH, w_b), lambda b, i: (0, 0)),
            pl.BlockSpec((1, w_b), lambda b, i: (0, 0)),
            pl.BlockSpec((1, w_b), lambda b, i: (0, 0)),
            pl.BlockSpec((1, w_b), lambda b, i: (0, 0)),
            pl.BlockSpec((ts, w_b), lambda b, i: (b * n_s + i, gate_col)),
        ],
        out_specs=pl.BlockSpec((ts, w_b), lambda b, i: (b * n_s + i, 0)),
        out_shape=jax.ShapeDtypeStruct((batch * seq, w_b), BF16),
        scratch_shapes=[pltpu.VMEM((w_b // V7X_LANES, CONV_HALO + ts, V7X_LANES), F32),
                        pltpu.VMEM((ts, w_b), F32)],
        compiler_params=_cparams("parallel", "arbitrary"),
        name="conv_module",
    )(pa, pa, conv_w.astype(F32), conv_b.reshape(1, w_b).astype(F32), ln_g.reshape(1, w_b).astype(F32),
      ln_b.reshape(1, w_b).astype(F32), pa)


def _sgu_kernel(ug_ref, v_ref, lg_ref, lb_ref, ws_ref, bs_ref, o_ref):
    tb, width = v_ref.shape
    gw = width // N_GROUPS_C
    pos_r = lax.broadcasted_iota(jnp.int32, (GMLP_CHUNK, GMLP_CHUNK), 0) // CHUNK
    pos_c = lax.broadcasted_iota(jnp.int32, (GMLP_CHUNK, GMLP_CHUNK), 1) // CHUNK
    causal = pos_r >= pos_c
    for n in range(tb // GMLP_CHUNK):
        rows = slice(n * GMLP_CHUNK, (n + 1) * GMLP_CHUNK)
        v = v_ref[rows, :].astype(F32)
        mu = jnp.mean(v, axis=-1, keepdims=True)
        d = v - mu
        var = jnp.mean(d * d, axis=-1, keepdims=True)
        vn = ((d * lax.rsqrt(var + EPS)) * lg_ref[...] + lb_ref[...]).astype(BF16)
        for g in range(N_GROUPS_C):
            cols = slice(g * gw, (g + 1) * gw)
            ws = jnp.where(causal, ws_ref[g], 0.0).astype(BF16)
            sg = _dot(ws, vn[:, cols]) + bs_ref[g]
            o_ref[rows, cols] = (ug_ref[rows, cols].astype(F32) * sg).astype(o_ref.dtype)


def _sgu(uv, ln_g, ln_b, w_s, b_s, *, width, tb):
    m = uv.shape[0]
    assert m % tb == 0 and tb % GMLP_CHUNK == 0
    return pl.pallas_call(
        _sgu_kernel,
        grid=(m // tb,),
        in_specs=[
            pl.BlockSpec((tb, width), lambda i: (i, 0)),
            pl.BlockSpec((tb, width), lambda i: (i, 1)),
            pl.BlockSpec((1, width), lambda i: (0, 0)),
            pl.BlockSpec((1, width), lambda i: (0, 0)),
            pl.BlockSpec((N_GROUPS_C, GMLP_CHUNK, GMLP_CHUNK), lambda i: (0, 0, 0)),
            pl.BlockSpec((N_GROUPS_C, GMLP_CHUNK, 1), lambda i: (0, 0, 0)),
        ],
        out_specs=pl.BlockSpec((tb, width), lambda i: (i, 0)),
        out_shape=jax.ShapeDtypeStruct((m, width), BF16),
        compiler_params=_cparams("parallel"),
        name="sgu",
    )(uv, uv, ln_g.reshape(1, width).astype(F32), ln_b.reshape(1, width).astype(F32),
      w_s.astype(F32), b_s.astype(F32)[..., None])


def _xattn_kernel(h_ref, g_ref, wq_ref, k_ref, v_ref, wo_ref, fg_ref, o_ref, hn_ref, q_ref, ctx_ref,
                  *, scale, final_norm):
    d = h_ref.shape[1]
    dh = d // N_HEADS_X
    x = h_ref[...]
    ms = jnp.mean(x * x, axis=-1, keepdims=True)
    hn_ref[...] = ((x * lax.rsqrt(ms + EPS)) * g_ref[...]).astype(BF16)
    q_ref[...] = (_dot(hn_ref[...], wq_ref[...]) * scale).astype(BF16)
    for h in range(N_HEADS_X):
        cols = slice(h * dh, (h + 1) * dh)
        s = _dot_nt(q_ref[:, cols], k_ref[:, cols])
        p = jnp.exp(s - jnp.max(s, axis=-1, keepdims=True))
        denom = jnp.sum(p, axis=-1, keepdims=True)
        ctx_ref[:, cols] = (_dot(p.astype(BF16), v_ref[:, cols]) / denom).astype(BF16)
    h_new = h_ref[...] + _dot(ctx_ref[...], wo_ref[...])
    if final_norm:
        ms = jnp.mean(h_new * h_new, axis=-1, keepdims=True)
        h_new = (h_new * lax.rsqrt(ms + EPS)) * fg_ref[...]
    o_ref[...] = h_new


def _cross_attention(h, g, wq, kv, wo, final_g, *, batch, seq, n_mem, tq, final_norm):
    m, d = h.shape
    assert seq % tq == 0
    n_s = seq // tq
    kern = functools.partial(_xattn_kernel, scale=(d // N_HEADS_X) ** -0.5, final_norm=final_norm)
    resident = dict(pipeline_mode=pl.Buffered(1))
    return pl.pallas_call(
        kern,
        grid=(batch, n_s),
        in_specs=[
            pl.BlockSpec((tq, d), lambda b, i: (b * n_s + i, 0)),
            pl.BlockSpec((1, d), lambda b, i: (0, 0)),
            pl.BlockSpec((d, d), lambda b, i: (0, 0), **resident),
            pl.BlockSpec((n_mem, d), lambda b, i: (b, 0)),
            pl.BlockSpec((n_mem, d), lambda b, i: (b, 1)),
            pl.BlockSpec((d, d), lambda b, i: (0, 0), **resident),
            pl.BlockSpec((1, d), lambda b, i: (0, 0)),
        ],
        out_specs=pl.BlockSpec((tq, d), lambda b, i: (b * n_s + i, 0)),
        out_shape=jax.ShapeDtypeStruct((m, d), F32),
        scratch_shapes=[pltpu.VMEM((tq, d), BF16), pltpu.VMEM((tq, d), BF16), pltpu.VMEM((tq, d), BF16)],
        compiler_params=_cparams("parallel", "arbitrary"),
        name="cross_attention",
    )(h, g.reshape(1, d).astype(F32), wq, kv, kv, wo, final_g.reshape(1, d).astype(F32))


def _pick(n, *cands):
    for c in cands:
        if n % c == 0:
            return c
    raise ValueError(f"no tile for {n} among {cands}")


def kernel(x, mem, norm_mix_g, norm_x_g, norm_mem_g, final_norm_g, w_in_ab, rel_bias, conv_w, conv_b,
           conv_ln_g, conv_ln_b, w_out_ab, w_in_c, sgu_ln_g, sgu_ln_b, w_s, b_s, w_out_c, w_xq, w_xk,
           w_xv, w_xo):
    batch, seq, d = x.shape
    n_mem = mem.shape[1]
    depth = norm_mix_g.shape[0]
    mix = w_out_ab.shape[1]
    w_a = mix // 2
    w_b = mix - w_a
    t = batch * seq
    tm = _pick(t, 1024, 512, 256)
    tn = _pick(w_a, 1024, 512, 256)
    nb_a, nb_b, nb_mix, nb_d = w_a // tn, w_b // tn, mix // tn, d // tn
    scale_a = HEAD_DIM_A ** -0.5

    h = x.reshape(t, d).astype(F32)
    mem2 = mem.reshape(batch * n_mem, d).astype(F32)
    for layer in range(depth):
        i = layer // 2
        if layer % 2 == 0:
            segs = [
                (nb_a, "plain", scale_a, 0, None),
                (2 * nb_a, "plain", 1.0, nb_a, None),
                (nb_b, "glu", 1.0, 3 * nb_a, 3 * nb_a + nb_b),
                (nb_mix, "silu", 1.0, 3 * nb_a + 2 * nb_b, None),
            ]
            pa = _norm_matmul(h, norm_mix_g[layer], w_in_ab[i].astype(BF16), segs, tm=tm, tn=tn)
            gate_col = 3 * w_a + w_b
            bias = _bias_tables(rel_bias[i])
            ya = _band_attention(pa, bias, batch=batch, seq=seq, w_a=w_a, gate_col=gate_col)
            yb = _conv_module(pa, conv_w[i], conv_b[i], conv_ln_g[i], conv_ln_b[i], batch=batch, seq=seq,
                              w_b=w_b, x_col=3 * w_a // w_b, gate_col=(gate_col + w_a) // w_b,
                              ts=_pick(seq, 256, 128))
            h = _resid_matmul(ya, 0, yb, 0, w_out_ab[i].astype(BF16), h, tm=tm, tn=tn)
        else:
            segs = [
                (nb_mix, "mulsilu", 1.0, 0, 2 * nb_mix),
                (nb_mix, "plain", 1.0, nb_mix, None),
            ]
            uv = _norm_matmul(h, norm_mix_g[layer], w_in_c[i].astype(BF16), segs, tm=tm, tn=tn)
            y = _sgu(uv, sgu_ln_g[i], sgu_ln_b[i], w_s[i], b_s[i], width=mix, tb=_pick(t, 256, 128))
            h = _resid_matmul(y, 0, y, 1, w_out_c[i].astype(BF16), h, tm=tm, tn=tn)

        w_kv = jnp.concatenate([w_xk[layer], w_xv[layer]], axis=1).astype(BF16)
        kv = _norm_matmul(mem2, norm_mem_g[layer], w_kv, [(2 * nb_d, "plain", 1.0, 0, None)],
                          tm=_pick(batch * n_mem, 512, 256, 128), tn=tn)
        h = _cross_attention(h, norm_x_g[layer], w_xq[layer].astype(BF16), kv, w_xo[layer].astype(BF16),
                             final_norm_g, batch=batch, seq=seq, n_mem=n_mem, tq=_pick(seq, 512, 256, 128),
                             final_norm=(layer == depth - 1))
    return h.reshape(batch, seq, d)
```

```python
import functools

import jax
import jax.numpy as jnp
from jax import lax
from jax.experimental import pallas as pl
from jax.experimental.pallas import tpu as pltpu

F32 = jnp.float32
BF16 = jnp.bfloat16

CHUNK = 64
N_PAST_CHUNKS = 8
MAX_REL = 128
HEAD_DIM_A = 128
CONV_WIDTH = 31
GMLP_CHUNK = 128
N_GROUPS_C = 8
N_HEADS_X = 4
EPS = 1e-6
NEG_INF = -1e30

V7X_LANES = 128
V7X_VMEM_LIMIT_BYTES = 56 * 1024 * 1024

ATT_QB = 2 * CHUNK
ATT_WIN = N_PAST_CHUNKS * CHUNK + ATT_QB
ATT_ROLL_WIDTH = 768
ATT_N_VARIANTS = N_PAST_CHUNKS * CHUNK // ATT_QB + 1
REL_ROWS = 384
CONV_HALO = 32
ATT_SLAB = 2 * HEAD_DIM_A


def _cparams(*sem):
    return pltpu.CompilerParams(dimension_semantics=sem, vmem_limit_bytes=V7X_VMEM_LIMIT_BYTES)


def _sigmoid(x):
    return 0.5 * jnp.tanh(0.5 * x) + 0.5


def _dot(a, b):
    return jnp.dot(a, b, preferred_element_type=F32)


def _dot_nt(a, b):
    return lax.dot_general(a, b, (((1,), (1,)), ((), ())), preferred_element_type=F32)


def _norm_matmul_kernel(a_tab, b_tab, x_ref, g_ref, wa_ref, wb_ref, o_ref, hn_ref, *, segments):
    del a_tab, b_tab
    j = pl.program_id(1)
    rows_per_iter = 16

    @pl.when(j == 0)
    def _():
        g = g_ref[...]

        def body(r, carry):
            rows = pl.ds(pl.multiple_of(r * rows_per_iter, rows_per_iter), rows_per_iter)
            x = x_ref[rows, :].astype(F32)
            ms = jnp.mean(x * x, axis=-1, keepdims=True)
            hn_ref[rows, :] = ((x * lax.rsqrt(ms + EPS)) * g).astype(BF16)
            return carry

        lax.fori_loop(0, x_ref.shape[0] // rows_per_iter, body, 0, unroll=4)

    lo = 0
    for n_steps, kind, scale in segments:
        hi = lo + n_steps

        @pl.when((j >= lo) & (j < hi))
        def _(kind=kind, scale=scale):
            a = _dot(hn_ref[...], wa_ref[...])
            if kind == "plain":
                r = a if scale == 1.0 else a * scale
            elif kind == "silu":
                r = a * _sigmoid(a)
            elif kind == "glu":
                r = a * _sigmoid(_dot(hn_ref[...], wb_ref[...]))
            elif kind == "mulsilu":
                b = _dot(hn_ref[...], wb_ref[...])
                r = a * (b * _sigmoid(b))
            else:
                raise ValueError(kind)
            if len(o_ref.shape) == 2:
                o_ref[...] = r.astype(o_ref.dtype)
            else:
                slab = o_ref.shape[2]
                for c in range(o_ref.shape[0]):
                    o_ref[c] = r[:, c * slab:(c + 1) * slab].astype(o_ref.dtype)

        lo = hi


def _norm_matmul(x, g, w, segments, *, tm, tn, slab=None):
    m, k = x.shape
    assert m % tm == 0 and w.shape[0] == k and w.shape[1] % tn == 0
    a_tab, b_tab = [], []
    for n_steps, _, _, a0, b0 in segments:
        for s in range(n_steps):
            a_tab.append(a0 + s)
            b_tab.append(None if b0 is None else b0 + s)
    known = [b for b in b_tab if b is not None]
    fill = known[0] if known else 0
    for idx, b in enumerate(b_tab):
        if b is None:
            b_tab[idx] = fill
        else:
            fill = b
    n_steps_total = len(a_tab)
    kern = functools.partial(_norm_matmul_kernel, segments=tuple(s[:3] for s in segments))
    if slab is None:
        out_spec = pl.BlockSpec((tm, tn), lambda i, j, at, bt: (i, j))
        out_shape = jax.ShapeDtypeStruct((m, n_steps_total * tn), BF16)
    else:
        assert tn % slab == 0
        out_spec = pl.BlockSpec((tn // slab, tm, slab), lambda i, j, at, bt: (j, i, 0))
        out_shape = jax.ShapeDtypeStruct((n_steps_total * tn // slab, m, slab), BF16)
    grid_spec = pltpu.PrefetchScalarGridSpec(
        num_scalar_prefetch=2,
        grid=(m // tm, n_steps_total),
        in_specs=[
            pl.BlockSpec((tm, k), lambda i, j, at, bt: (i, 0)),
            pl.BlockSpec((1, k), lambda i, j, at, bt: (0, 0)),
            pl.BlockSpec((k, tn), lambda i, j, at, bt: (0, at[j])),
            pl.BlockSpec((k, tn), lambda i, j, at, bt: (0, bt[j])),
        ],
        out_specs=out_spec,
        scratch_shapes=[pltpu.VMEM((tm, k), BF16)],
    )
    return pl.pallas_call(
        kern,
        grid_spec=grid_spec,
        out_shape=out_shape,
        compiler_params=_cparams("parallel", "arbitrary"),
        name="norm_matmul",
    )(jnp.asarray(a_tab, jnp.int32), jnp.asarray(b_tab, jnp.int32), x, g.reshape(1, k), w, w)


def _resid_matmul_kernel(l0_ref, l1_ref, w0_ref, w1_ref, r_ref, o_ref):
    acc = _dot(l0_ref[...], w0_ref[...]) + _dot(l1_ref[...], w1_ref[...])
    o_ref[...] = r_ref[...] + acc


def _resid_matmul(l0, c0, l1, c1, w, resid, *, tm, tn):
    m, n = resid.shape
    kh = w.shape[0] // 2
    assert m % tm == 0 and n % tn == 0
    return pl.pallas_call(
        _resid_matmul_kernel,
        grid=(m // tm, n // tn),
        in_specs=[
            pl.BlockSpec((tm, kh), lambda i, j: (i, c0)),
            pl.BlockSpec((tm, kh), lambda i, j: (i, c1)),
            pl.BlockSpec((kh, tn), lambda i, j: (0, j)),
            pl.BlockSpec((kh, tn), lambda i, j: (1, j)),
            pl.BlockSpec((tm, tn), lambda i, j: (i, j)),
        ],
        out_specs=pl.BlockSpec((tm, tn), lambda i, j: (i, j)),
        out_shape=jax.ShapeDtypeStruct((m, n), F32),
        compiler_params=_cparams("parallel", "arbitrary"),
        name="resid_matmul",
    )(l0, l1, w, w, resid)


def _bias_table_kernel(rb_ref, o_ref, *, n_heads):
    t = pl.program_id(0)
    width = ATT_ROLL_WIDTH
    m = lax.broadcasted_iota(jnp.int32, (REL_ROWS, width), 1)
    r = lax.broadcasted_iota(jnp.int32, (REL_ROWS, width), 0)
    mm = jnp.where(m < ATT_WIN, m, m - width)
    idx = jnp.clip(ATT_QB * t - mm, -MAX_REL, MAX_REL) + MAX_REL
    onehot = jnp.where(r == idx, 1.0, 0.0).astype(BF16)
    rb = rb_ref[...]
    hi = rb.astype(BF16)
    rem = rb - hi.astype(F32)
    mid = rem.astype(BF16)
    lo = (rem - mid.astype(F32)).astype(BF16)
    u = _dot(hi, onehot) + _dot(mid, onehot) + _dot(lo, onehot)

    qi = lax.broadcasted_iota(jnp.int32, (ATT_QB, ATT_WIN), 0) // CHUNK
    kj = lax.broadcasted_iota(jnp.int32, (ATT_QB, ATT_WIN), 1) // CHUNK
    q_chunk = (ATT_QB // CHUNK) * t + qi
    valid = (kj <= q_chunk) & (kj >= q_chunk - N_PAST_CHUNKS)
    for h in range(n_heads):
        x = jnp.broadcast_to(u[h:h + 1, :], (ATT_QB, width))
        toeplitz = pltpu.roll(x, 0, 1, stride=1, stride_axis=0)
        o_ref[0, h] = jnp.where(valid, toeplitz[:, :ATT_WIN], NEG_INF)


def _bias_tables(rel_bias):
    n_heads, n_rel = rel_bias.shape
    rb = jnp.pad(rel_bias.astype(F32), ((0, 0), (0, REL_ROWS - n_rel)))
    return pl.pallas_call(
        functools.partial(_bias_table_kernel, n_heads=n_heads),
        grid=(ATT_N_VARIANTS,),
        in_specs=[pl.BlockSpec((n_heads, REL_ROWS), lambda t: (0, 0))],
        out_specs=pl.BlockSpec((1, n_heads, ATT_QB, ATT_WIN), lambda t: (t, 0, 0, 0)),
        out_shape=jax.ShapeDtypeStruct((ATT_N_VARIANTS, n_heads, ATT_QB, ATT_WIN), F32),
        compiler_params=_cparams("arbitrary"),
        name="bias_tables",
    )(rb)


def _band_attn_kernel(q_ref, k_ref, v_ref, bias_ref, gate_ref, o_ref,
                      s_even, s_odd, p_even, p_odd, l_even, l_odd, *, heads_per_step):
    dh = HEAD_DIM_A
    n_blocks = q_ref.shape[0] // ATT_QB
    last_variant = ATT_N_VARIANTS - 1
    heads = [(h, slice(h * dh, (h + 1) * dh)) for h in range(heads_per_step)]

    def window(blk):
        return pl.ds(pl.multiple_of(jnp.maximum(blk - last_variant, 0) * ATT_QB, ATT_QB), ATT_WIN)

    def block_rows(blk):
        return pl.ds(pl.multiple_of(blk * ATT_QB, ATT_QB), ATT_QB)

    def scores(blk, s_buf):
        variant = jnp.minimum(blk, last_variant)
        for h, cols in heads:
            s_buf[h] = _dot_nt(q_ref[block_rows(blk), cols], k_ref[window(blk), cols]) + bias_ref[variant, h]

    def exponentiate(s_buf, p_buf, l_buf):
        for h, _ in heads:
            s = s_buf[h]
            p = jnp.exp(s - jnp.max(s, axis=-1, keepdims=True))
            l_buf[h] = jnp.sum(p, axis=-1, keepdims=True)
            p_buf[h] = p.astype(BF16)

    def apply_values(blk, p_buf, l_buf):
        rows = block_rows(blk)
        for h, cols in heads:
            o = _dot(p_buf[h], v_ref[window(blk), cols]) / l_buf[h]
            o_ref[rows, cols] = (o * gate_ref[rows, cols].astype(F32)).astype(o_ref.dtype)

    scores(0, s_even)
    scores(1, s_odd)
    exponentiate(s_even, p_even, l_even)

    def pair(t, carry):
        m = 2 * t + 1
        scores(m + 1, s_even)
        exponentiate(s_odd, p_odd, l_odd)
        apply_values(m - 1, p_even, l_even)
        scores(m + 2, s_odd)
        exponentiate(s_even, p_even, l_even)
        apply_values(m, p_odd, l_odd)
        return carry

    lax.fori_loop(0, (n_blocks - 2) // 2, pair, 0)
    exponentiate(s_odd, p_odd, l_odd)
    apply_values(n_blocks - 2, p_even, l_even)
    apply_values(n_blocks - 1, p_odd, l_odd)


def _band_attention(pa, bias, *, batch, seq, w_a, gate_col):
    heads_per_step = ATT_SLAB // HEAD_DIM_A
    cw = ATT_SLAB
    n_blocks = seq // ATT_QB
    assert seq % ATT_QB == 0 and n_blocks % 2 == 0 and n_blocks >= 4
    assert w_a % cw == 0 and gate_col % cw == 0 and pa.shape[2] == cw
    kern = functools.partial(_band_attn_kernel, heads_per_step=heads_per_step)
    seq_cols = lambda first: pl.BlockSpec((None, seq, cw), lambda b, g: (first // cw + g, b, 0))
    return pl.pallas_call(
        kern,
        grid=(batch, w_a // cw),
        in_specs=[
            seq_cols(0),
            seq_cols(w_a),
            seq_cols(2 * w_a),
            pl.BlockSpec((ATT_N_VARIANTS, heads_per_step, ATT_QB, ATT_WIN), lambda b, g: (0, g, 0, 0)),
            seq_cols(gate_col),
        ],
        out_specs=pl.BlockSpec((seq, cw), lambda b, g: (b, g)),
        out_shape=jax.ShapeDtypeStruct((batch * seq, w_a), BF16),
        scratch_shapes=[
            pltpu.VMEM((heads_per_step, ATT_QB, ATT_WIN), F32),
            pltpu.VMEM((heads_per_step, ATT_QB, ATT_WIN), F32),
            pltpu.VMEM((heads_per_step, ATT_QB, ATT_WIN), BF16),
            pltpu.VMEM((heads_per_step, ATT_QB, ATT_WIN), BF16),
            pltpu.VMEM((heads_per_step, ATT_QB, 1), F32),
            pltpu.VMEM((heads_per_step, ATT_QB, 1), F32),
        ],
        compiler_params=_cparams("parallel", "parallel"),
        name="band_attention",
    )(pa, pa, pa, bias, pa)


def _conv_module_kernel(x_ref, halo_ref, w_ref, cb_ref, lg_ref, lb_ref, gate_ref, o_ref, xw_ref, y_ref):
    n_slabs, ts, slab = x_ref.shape
    c = n_slabs * slab
    i = pl.program_id(1)
    n_lane_blocks = c // V7X_LANES
    lbs_per_slab = slab // V7X_LANES

    def lanes_of(lb):
        return lb // lbs_per_slab, slice((lb % lbs_per_slab) * V7X_LANES, (lb % lbs_per_slab + 1) * V7X_LANES)

    for lb in range(n_lane_blocks):
        s, cols = lanes_of(lb)
        xw_ref[lb, CONV_HALO:CONV_HALO + ts, :] = x_ref[s, :, cols].astype(F32)

    @pl.when(i == 0)
    def _():
        xw_ref[:, 0:CONV_HALO, :] = jnp.zeros((n_lane_blocks, CONV_HALO, V7X_LANES), F32)

    @pl.when(i > 0)
    def _():
        for lb in range(n_lane_blocks):
            s, cols = lanes_of(lb)
            xw_ref[lb, 0:CONV_HALO, :] = halo_ref[s, :, cols].astype(F32)

    row_blk = 128
    sub = 8
    first_tap = CONV_HALO - (CONV_WIDTH - 1)

    def conv_body(it, carry):
        rb = it // n_lane_blocks
        lb = it % n_lane_blocks
        cols = pl.ds(pl.multiple_of(lb * V7X_LANES, V7X_LANES), V7X_LANES)
        base = rb * row_blk
        acc = [jnp.broadcast_to(cb_ref[:, cols], (sub, V7X_LANES))] * (row_blk // sub)
        for tap in range(CONV_WIDTH):
            wk = jnp.broadcast_to(w_ref[tap:tap + 1, cols], (sub, V7X_LANES))
            acc = [a + wk * xw_ref[lb, pl.ds(base + r * sub + first_tap + tap, sub), :]
                   for r, a in enumerate(acc)]
        for r, a in enumerate(acc):
            y_ref[pl.ds(pl.multiple_of(base + r * sub, sub), sub), cols] = a
        return carry

    lax.fori_loop(0, (ts // row_blk) * n_lane_blocks, conv_body, 0, unroll=2)

    ln_rows = 16

    def ln_body(r, carry):
        rows = pl.ds(pl.multiple_of(r * ln_rows, ln_rows), ln_rows)
        y = y_ref[rows, :]
        mu = jnp.mean(y, axis=-1, keepdims=True)
        d = y - mu
        var = jnp.mean(d * d, axis=-1, keepdims=True)
        z = (d * lax.rsqrt(var + EPS)) * lg_ref[...] + lb_ref[...]
        z = z * _sigmoid(z)
        gate = jnp.concatenate([gate_ref[s, rows, :] for s in range(n_slabs)], axis=1)
        o_ref[rows, :] = (z * gate.astype(F32)).astype(o_ref.dtype)
        return carry

    lax.fori_loop(0, ts // ln_rows, ln_body, 0, unroll=8)


def _conv_module(pa, conv_w, conv_b, ln_g, ln_b, *, batch, seq, w_b, x_col, gate_col, ts):
    slab = pa.shape[2]
    n_slabs = w_b // slab
    assert seq % ts == 0 and ts % 128 == 0 and ts % CONV_HALO == 0
    assert w_b % slab == 0 and x_col % w_b == 0 and gate_col % w_b == 0
    n_s = seq // ts
    halo_per_blk = ts // CONV_HALO
    return pl.pallas_call(
        _conv_module_kernel,
        grid=(batch, n_s),
        in_specs=[
            pl.BlockSpec((n_slabs, ts, slab), lambda b, i: (x_col // w_b, b * n_s + i, 0)),
            pl.BlockSpec((n_slabs, CONV_HALO, slab),
                         lambda b, i: (x_col // w_b, jnp.maximum((b * n_s + i) * halo_per_blk - 1, 0), 0)),
            pl.BlockSpec((CONV_WIDTH, w_b), lambda b, i: (0, 0)),
            pl.BlockSpec((1, w_b), lambda b, i: (0, 0)),
            pl.BlockSpec((1, w_b), lambda b, i: (0, 0)),
            pl.BlockSpec((1, w_b), lambda b, i: (0, 0)),
            pl.BlockSpec((n_slabs, ts, slab), lambda b, i: (gate_col // w_b, b * n_s + i, 0)),
        ],
        out_specs=pl.BlockSpec((ts, w_b), lambda b, i: (b * n_s + i, 0)),
        out_shape=jax.ShapeDtypeStruct((batch * seq, w_b), BF16),
        scratch_shapes=[pltpu.VMEM((w_b // V7X_LANES, CONV_HALO + ts, V7X_LANES), F32),
                        pltpu.VMEM((ts, w_b), F32)],
        compiler_params=_cparams("parallel", "arbitrary"),
        name="conv_module",
    )(pa, pa, conv_w.astype(F32), conv_b.reshape(1, w_b).astype(F32), ln_g.reshape(1, w_b).astype(F32),
      ln_b.reshape(1, w_b).astype(F32), pa)


def _sgu_kernel(ug_ref, v_ref, lg_ref, lb_ref, ws_ref, bs_ref, o_ref):
    tb, width = v_ref.shape
    gw = width // N_GROUPS_C
    pos_r = lax.broadcasted_iota(jnp.int32, (GMLP_CHUNK, GMLP_CHUNK), 0) // CHUNK
    pos_c = lax.broadcasted_iota(jnp.int32, (GMLP_CHUNK, GMLP_CHUNK), 1) // CHUNK
    causal = pos_r >= pos_c
    for n in range(tb // GMLP_CHUNK):
        rows = slice(n * GMLP_CHUNK, (n + 1) * GMLP_CHUNK)
        v = v_ref[rows, :].astype(F32)
        mu = jnp.mean(v, axis=-1, keepdims=True)
        d = v - mu
        var = jnp.mean(d * d, axis=-1, keepdims=True)
        vn = ((d * lax.rsqrt(var + EPS)) * lg_ref[...] + lb_ref[...]).astype(BF16)
        for g in range(N_GROUPS_C):
            cols = slice(g * gw, (g + 1) * gw)
            ws = jnp.where(causal, ws_ref[g], 0.0).astype(BF16)
            sg = _dot(ws, vn[:, cols]) + bs_ref[g]
            o_ref[rows, cols] = (ug_ref[rows, cols].astype(F32) * sg).astype(o_ref.dtype)


def _sgu(uv, ln_g, ln_b, w_s, b_s, *, width, tb):
    m = uv.shape[0]
    assert m % tb == 0 and tb % GMLP_CHUNK == 0
    return pl.pallas_call(
        _sgu_kernel,
        grid=(m // tb,),
        in_specs=[
            pl.BlockSpec((tb, width), lambda i: (i, 0)),
            pl.BlockSpec((tb, width), lambda i: (i, 1)),
            pl.BlockSpec((1, width), lambda i: (0, 0)),
            pl.BlockSpec((1, width), lambda i: (0, 0)),
            pl.BlockSpec((N_GROUPS_C, GMLP_CHUNK, GMLP_CHUNK), lambda i: (0, 0, 0)),
            pl.BlockSpec((N_GROUPS_C, GMLP_CHUNK, 1), lambda i: (0, 0, 0)),
        ],
        out_specs=pl.BlockSpec((tb, width), lambda i: (i, 0)),
        out_shape=jax.ShapeDtypeStruct((m, width), BF16),
        compiler_params=_cparams("parallel"),
        name="sgu",
    )(uv, uv, ln_g.reshape(1, width).astype(F32), ln_b.reshape(1, width).astype(F32),
      w_s.astype(F32), b_s.astype(F32)[..., None])


def _xattn_kernel(h_ref, g_ref, wq_ref, k_ref, v_ref, wo_ref, fg_ref, o_ref, hn_ref, q_ref, ctx_ref,
                  *, scale, final_norm):
    d = h_ref.shape[1]
    dh = d // N_HEADS_X
    x = h_ref[...]
    ms = jnp.mean(x * x, axis=-1, keepdims=True)
    hn_ref[...] = ((x * lax.rsqrt(ms + EPS)) * g_ref[...]).astype(BF16)
    q_ref[...] = (_dot(hn_ref[...], wq_ref[...]) * scale).astype(BF16)
    for h in range(N_HEADS_X):
        cols = slice(h * dh, (h + 1) * dh)
        s = _dot_nt(q_ref[:, cols], k_ref[:, cols])
        p = jnp.exp(s - jnp.max(s, axis=-1, keepdims=True))
        denom = jnp.sum(p, axis=-1, keepdims=True)
        ctx_ref[:, cols] = (_dot(p.astype(BF16), v_ref[:, cols]) / denom).astype(BF16)
    h_new = h_ref[...] + _dot(ctx_ref[...], wo_ref[...])
    if final_norm:
        ms = jnp.mean(h_new * h_new, axis=-1, keepdims=True)
        h_new = (h_new * lax.rsqrt(ms + EPS)) * fg_ref[...]
    o_ref[...] = h_new


def _cross_attention(h, g, wq, kv, wo, final_g, *, batch, seq, n_mem, tq, final_norm):
    m, d = h.shape
    assert seq % tq == 0
    n_s = seq // tq
    kern = functools.partial(_xattn_kernel, scale=(d // N_HEADS_X) ** -0.5, final_norm=final_norm)
    resident = dict(pipeline_mode=pl.Buffered(1))
    return pl.pallas_call(
        kern,
        grid=(batch, n_s),
        in_specs=[
            pl.BlockSpec((tq, d), lambda b, i: (b * n_s + i, 0)),
            pl.BlockSpec((1, d), lambda b, i: (0, 0)),
            pl.BlockSpec((d, d), lambda b, i: (0, 0), **resident),
            pl.BlockSpec((n_mem, d), lambda b, i: (b, 0)),
            pl.BlockSpec((n_mem, d), lambda b, i: (b, 1)),
            pl.BlockSpec((d, d), lambda b, i: (0, 0), **resident),
            pl.BlockSpec((1, d), lambda b, i: (0, 0)),
        ],
        out_specs=pl.BlockSpec((tq, d), lambda b, i: (b * n_s + i, 0)),
        out_shape=jax.ShapeDtypeStruct((m, d), F32),
        scratch_shapes=[pltpu.VMEM((tq, d), BF16), pltpu.VMEM((tq, d), BF16), pltpu.VMEM((tq, d), BF16)],
        compiler_params=_cparams("parallel", "arbitrary"),
        name="cross_attention",
    )(h, g.reshape(1, d).astype(F32), wq, kv, kv, wo, final_g.reshape(1, d).astype(F32))


def _pick(n, *cands):
    for c in cands:
        if n % c == 0:
            return c
    raise ValueError(f"no tile for {n} among {cands}")


def kernel(x, mem, norm_mix_g, norm_x_g, norm_mem_g, final_norm_g, w_in_ab, rel_bias, conv_w, conv_b,
           conv_ln_g, conv_ln_b, w_out_ab, w_in_c, sgu_ln_g, sgu_ln_b, w_s, b_s, w_out_c, w_xq, w_xk,
           w_xv, w_xo):
    batch, seq, d = x.shape
    n_mem = mem.shape[1]
    depth = norm_mix_g.shape[0]
    mix = w_out_ab.shape[1]
    w_a = mix // 2
    w_b = mix - w_a
    t = batch * seq
    tm = _pick(t, 1024, 512, 256)
    tn = _pick(w_a, 1024, 512, 256)
    nb_a, nb_b, nb_mix, nb_d = w_a // tn, w_b // tn, mix // tn, d // tn
    scale_a = HEAD_DIM_A ** -0.5

    h = x.reshape(t, d).astype(F32)
    mem2 = mem.reshape(batch * n_mem, d).astype(F32)
    for layer in range(depth):
        i = layer // 2
        if layer % 2 == 0:
            segs = [
                (nb_a, "plain", scale_a, 0, None),
                (2 * nb_a, "plain", 1.0, nb_a, None),
                (nb_b, "glu", 1.0, 3 * nb_a, 3 * nb_a + nb_b),
                (nb_mix, "silu", 1.0, 3 * nb_a + 2 * nb_b, None),
            ]
            pa = _norm_matmul(h, norm_mix_g[layer], w_in_ab[i].astype(BF16), segs, tm=tm, tn=tn, slab=ATT_SLAB)
            gate_col = 3 * w_a + w_b
            bias = _bias_tables(rel_bias[i])
            ya = _band_attention(pa, bias, batch=batch, seq=seq, w_a=w_a, gate_col=gate_col)
            yb = _conv_module(pa, conv_w[i], conv_b[i], conv_ln_g[i], conv_ln_b[i], batch=batch, seq=seq,
                              w_b=w_b, x_col=3 * w_a, gate_col=gate_col + w_a, ts=_pick(seq, 256, 128))
            h = _resid_matmul(ya, 0, yb, 0, w_out_ab[i].astype(BF16), h, tm=tm, tn=tn)
        else:
            segs = [
                (nb_mix, "mulsilu", 1.0, 0, 2 * nb_mix),
                (nb_mix, "plain", 1.0, nb_mix, None),
            ]
            uv = _norm_matmul(h, norm_mix_g[layer], w_in_c[i].astype(BF16), segs, tm=tm, tn=tn)
            y = _sgu(uv, sgu_ln_g[i], sgu_ln_b[i], w_s[i], b_s[i], width=mix, tb=_pick(t, 256, 128))
            h = _resid_matmul(y, 0, y, 1, w_out_c[i].astype(BF16), h, tm=tm, tn=tn)

        w_kv = jnp.concatenate([w_xk[layer], w_xv[layer]], axis=1).astype(BF16)
        kv = _norm_matmul(mem2, norm_mem_g[layer], w_kv, [(2 * nb_d, "plain", 1.0, 0, None)],
                          tm=_pick(batch * n_mem, 512, 256, 128), tn=tn)
        h = _cross_attention(h, norm_x_g[layer], w_xq[layer].astype(BF16), kv, w_xo[layer].astype(BF16),
                             final_norm_g, batch=batch, seq=seq, n_mem=n_mem, tq=_pick(seq, 512, 256, 128),
                             final_norm=(layer == depth - 1))
    return h.reshape(batch, seq, d)
```

```python
import functools

import jax
import jax.numpy as jnp
from jax import lax
from jax.experimental import pallas as pl
from jax.experimental.pallas import tpu as pltpu

F32 = jnp.float32
BF16 = jnp.bfloat16

CHUNK = 64
N_PAST_CHUNKS = 8
MAX_REL = 128
HEAD_DIM_A = 128
CONV_WIDTH = 31
GMLP_CHUNK = 128
N_GROUPS_C = 8
N_HEADS_X = 4
EPS = 1e-6
NEG_INF = -1e30

V7X_LANES = 128
V7X_VMEM_LIMIT_BYTES = 56 * 1024 * 1024

ATT_QB = 2 * CHUNK
ATT_WIN = N_PAST_CHUNKS * CHUNK + ATT_QB
ATT_ROLL_WIDTH = 768
ATT_N_VARIANTS = N_PAST_CHUNKS * CHUNK // ATT_QB + 1
REL_ROWS = 384
CONV_HALO = 32
ATT_SLAB = 2 * HEAD_DIM_A


def _cparams(*sem):
    return pltpu.CompilerParams(dimension_semantics=sem, vmem_limit_bytes=V7X_VMEM_LIMIT_BYTES)


def _sigmoid(x):
    return 0.5 * jnp.tanh(0.5 * x) + 0.5


def _dot(a, b):
    return jnp.dot(a, b, preferred_element_type=F32)


def _dot_nt(a, b):
    return lax.dot_general(a, b, (((1,), (1,)), ((), ())), preferred_element_type=F32)


def _norm_matmul_kernel(a_tab, b_tab, x_ref, g_ref, wa_ref, wb_ref, o_ref, hn_ref, *, segments):
    del a_tab, b_tab
    j = pl.program_id(1)
    rows_per_iter = 16

    @pl.when(j == 0)
    def _():
        g = g_ref[...]

        def body(r, carry):
            rows = pl.ds(pl.multiple_of(r * rows_per_iter, rows_per_iter), rows_per_iter)
            x = x_ref[rows, :].astype(F32)
            ms = jnp.mean(x * x, axis=-1, keepdims=True)
            hn_ref[rows, :] = ((x * lax.rsqrt(ms + EPS)) * g).astype(BF16)
            return carry

        lax.fori_loop(0, x_ref.shape[0] // rows_per_iter, body, 0, unroll=8)

    lo = 0
    for n_steps, kind, scale in segments:
        hi = lo + n_steps

        @pl.when((j >= lo) & (j < hi))
        def _(kind=kind, scale=scale):
            a = _dot(hn_ref[...], wa_ref[...])
            if kind == "plain":
                r = a if scale == 1.0 else a * scale
            elif kind == "silu":
                r = a * _sigmoid(a)
            elif kind == "glu":
                r = a * _sigmoid(_dot(hn_ref[...], wb_ref[...]))
            elif kind == "mulsilu":
                b = _dot(hn_ref[...], wb_ref[...])
                r = a * (b * _sigmoid(b))
            else:
                raise ValueError(kind)
            if len(o_ref.shape) == 2:
                o_ref[...] = r.astype(o_ref.dtype)
            else:
                slab = o_ref.shape[2]
                for c in range(o_ref.shape[0]):
                    o_ref[c] = r[:, c * slab:(c + 1) * slab].astype(o_ref.dtype)

        lo = hi


def _norm_matmul(x, g, w, segments, *, tm, tn, slab=None):
    m, k = x.shape
    assert m % tm == 0 and w.shape[0] == k and w.shape[1] % tn == 0
    a_tab, b_tab = [], []
    for n_steps, _, _, a0, b0 in segments:
        for s in range(n_steps):
            a_tab.append(a0 + s)
            b_tab.append(None if b0 is None else b0 + s)
    known = [b for b in b_tab if b is not None]
    fill = known[0] if known else 0
    for idx, b in enumerate(b_tab):
        if b is None:
            b_tab[idx] = fill
        else:
            fill = b
    n_steps_total = len(a_tab)
    kern = functools.partial(_norm_matmul_kernel, segments=tuple(s[:3] for s in segments))
    if slab is None:
        out_spec = pl.BlockSpec((tm, tn), lambda i, j, at, bt: (i, j))
        out_shape = jax.ShapeDtypeStruct((m, n_steps_total * tn), BF16)
    else:
        assert tn % slab == 0
        out_spec = pl.BlockSpec((tn // slab, tm, slab), lambda i, j, at, bt: (j, i, 0))
        out_shape = jax.ShapeDtypeStruct((n_steps_total * tn // slab, m, slab), BF16)
    grid_spec = pltpu.PrefetchScalarGridSpec(
        num_scalar_prefetch=2,
        grid=(m // tm, n_steps_total),
        in_specs=[
            pl.BlockSpec((tm, k), lambda i, j, at, bt: (i, 0)),
            pl.BlockSpec((1, k), lambda i, j, at, bt: (0, 0)),
            pl.BlockSpec((k, tn), lambda i, j, at, bt: (0, at[j])),
            pl.BlockSpec((k, tn), lambda i, j, at, bt: (0, bt[j])),
        ],
        out_specs=out_spec,
        scratch_shapes=[pltpu.VMEM((tm, k), BF16)],
    )
    return pl.pallas_call(
        kern,
        grid_spec=grid_spec,
        out_shape=out_shape,
        compiler_params=_cparams("parallel", "arbitrary"),
        name="norm_matmul",
    )(jnp.asarray(a_tab, jnp.int32), jnp.asarray(b_tab, jnp.int32), x, g.reshape(1, k), w, w)


def _resid_matmul_kernel(l0_ref, l1_ref, w0_ref, w1_ref, r_ref, o_ref):
    acc = _dot(l0_ref[...], w0_ref[...]) + _dot(l1_ref[...], w1_ref[...])
    o_ref[...] = r_ref[...] + acc


def _resid_matmul(l0, c0, l1, c1, w, resid, *, tm, tn):
    m, n = resid.shape
    kh = w.shape[0] // 2
    assert m % tm == 0 and n % tn == 0
    return pl.pallas_call(
        _resid_matmul_kernel,
        grid=(m // tm, n // tn),
        in_specs=[
            pl.BlockSpec((tm, kh), lambda i, j: (i, c0)),
            pl.BlockSpec((tm, kh), lambda i, j: (i, c1)),
            pl.BlockSpec((kh, tn), lambda i, j: (0, j)),
            pl.BlockSpec((kh, tn), lambda i, j: (1, j)),
            pl.BlockSpec((tm, tn), lambda i, j: (i, j)),
        ],
        out_specs=pl.BlockSpec((tm, tn), lambda i, j: (i, j)),
        out_shape=jax.ShapeDtypeStruct((m, n), F32),
        compiler_params=_cparams("parallel", "arbitrary"),
        name="resid_matmul",
    )(l0, l1, w, w, resid)


def _bias_table_kernel(rb_ref, o_ref, *, n_heads):
    t = pl.program_id(0)
    width = ATT_ROLL_WIDTH
    m = lax.broadcasted_iota(jnp.int32, (REL_ROWS, width), 1)
    r = lax.broadcasted_iota(jnp.int32, (REL_ROWS, width), 0)
    mm = jnp.where(m < ATT_WIN, m, m - width)
    idx = jnp.clip(ATT_QB * t - mm, -MAX_REL, MAX_REL) + MAX_REL
    onehot = jnp.where(r == idx, 1.0, 0.0).astype(BF16)
    rb = rb_ref[...]
    hi = rb.astype(BF16)
    rem = rb - hi.astype(F32)
    mid = rem.astype(BF16)
    lo = (rem - mid.astype(F32)).astype(BF16)
    u = _dot(hi, onehot) + _dot(mid, onehot) + _dot(lo, onehot)

    qi = lax.broadcasted_iota(jnp.int32, (ATT_QB, ATT_WIN), 0) // CHUNK
    kj = lax.broadcasted_iota(jnp.int32, (ATT_QB, ATT_WIN), 1) // CHUNK
    q_chunk = (ATT_QB // CHUNK) * t + qi
    valid = (kj <= q_chunk) & (kj >= q_chunk - N_PAST_CHUNKS)
    for h in range(n_heads):
        x = jnp.broadcast_to(u[h:h + 1, :], (ATT_QB, width))
        toeplitz = pltpu.roll(x, 0, 1, stride=1, stride_axis=0)
        o_ref[0, h] = jnp.where(valid, toeplitz[:, :ATT_WIN], NEG_INF)


def _bias_tables(rel_bias):
    n_heads, n_rel = rel_bias.shape
    rb = jnp.pad(rel_bias.astype(F32), ((0, 0), (0, REL_ROWS - n_rel)))
    return pl.pallas_call(
        functools.partial(_bias_table_kernel, n_heads=n_heads),
        grid=(ATT_N_VARIANTS,),
        in_specs=[pl.BlockSpec((n_heads, REL_ROWS), lambda t: (0, 0))],
        out_specs=pl.BlockSpec((1, n_heads, ATT_QB, ATT_WIN), lambda t: (t, 0, 0, 0)),
        out_shape=jax.ShapeDtypeStruct((ATT_N_VARIANTS, n_heads, ATT_QB, ATT_WIN), F32),
        compiler_params=_cparams("arbitrary"),
        name="bias_tables",
    )(rb)


def _band_attn_kernel(q_ref, k_ref, v_ref, bias_ref, gate_ref, o_ref,
                      s_even, s_odd, p_even, p_odd, l_even, l_odd, *, heads_per_step):
    dh = HEAD_DIM_A
    n_blocks = q_ref.shape[0] // ATT_QB
    last_variant = ATT_N_VARIANTS - 1
    heads = [(h, slice(h * dh, (h + 1) * dh)) for h in range(heads_per_step)]

    def window(blk):
        return pl.ds(pl.multiple_of(jnp.maximum(blk - last_variant, 0) * ATT_QB, ATT_QB), ATT_WIN)

    def block_rows(blk):
        return pl.ds(pl.multiple_of(blk * ATT_QB, ATT_QB), ATT_QB)

    def scores(blk, s_buf):
        variant = jnp.minimum(blk, last_variant)
        for h, cols in heads:
            s_buf[h] = _dot_nt(q_ref[block_rows(blk), cols], k_ref[window(blk), cols]) + bias_ref[variant, h]

    def exponentiate(s_buf, p_buf, l_buf):
        for h, _ in heads:
            s = s_buf[h]
            p = jnp.exp(s - jnp.max(s, axis=-1, keepdims=True))
            l_buf[h] = jnp.sum(p, axis=-1, keepdims=True)
            p_buf[h] = p.astype(BF16)

    def apply_values(blk, p_buf, l_buf):
        rows = block_rows(blk)
        for h, cols in heads:
            o = _dot(p_buf[h], v_ref[window(blk), cols]) / l_buf[h]
            o_ref[rows, cols] = (o * gate_ref[rows, cols].astype(F32)).astype(o_ref.dtype)

    scores(0, s_even)
    scores(1, s_odd)
    exponentiate(s_even, p_even, l_even)

    def pair(t, carry):
        m = 2 * t + 1
        scores(m + 1, s_even)
        exponentiate(s_odd, p_odd, l_odd)
        apply_values(m - 1, p_even, l_even)
        scores(m + 2, s_odd)
        exponentiate(s_even, p_even, l_even)
        apply_values(m, p_odd, l_odd)
        return carry

    lax.fori_loop(0, (n_blocks - 2) // 2, pair, 0)
    exponentiate(s_odd, p_odd, l_odd)
    apply_values(n_blocks - 2, p_even, l_even)
    apply_values(n_blocks - 1, p_odd, l_odd)


def _band_attention(pa, bias, *, batch, seq, w_a, gate_col):
    heads_per_step = ATT_SLAB // HEAD_DIM_A
    cw = ATT_SLAB
    n_blocks = seq // ATT_QB
    assert seq % ATT_QB == 0 and n_blocks % 2 == 0 and n_blocks >= 4
    assert w_a % cw == 0 and gate_col % cw == 0 and pa.shape[2] == cw
    kern = functools.partial(_band_attn_kernel, heads_per_step=heads_per_step)
    seq_cols = lambda first: pl.BlockSpec((None, seq, cw), lambda b, g: (first // cw + g, b, 0))
    return pl.pallas_call(
        kern,
        grid=(batch, w_a // cw),
        in_specs=[
            seq_cols(0),
            seq_cols(w_a),
            seq_cols(2 * w_a),
            pl.BlockSpec((ATT_N_VARIANTS, heads_per_step, ATT_QB, ATT_WIN), lambda b, g: (0, g, 0, 0)),
            seq_cols(gate_col),
        ],
        out_specs=pl.BlockSpec((seq, cw), lambda b, g: (b, g)),
        out_shape=jax.ShapeDtypeStruct((batch * seq, w_a), BF16),
        scratch_shapes=[
            pltpu.VMEM((heads_per_step, ATT_QB, ATT_WIN), F32),
            pltpu.VMEM((heads_per_step, ATT_QB, ATT_WIN), F32),
            pltpu.VMEM((heads_per_step, ATT_QB, ATT_WIN), BF16),
            pltpu.VMEM((heads_per_step, ATT_QB, ATT_WIN), BF16),
            pltpu.VMEM((heads_per_step, ATT_QB, 1), F32),
            pltpu.VMEM((heads_per_step, ATT_QB, 1), F32),
        ],
        compiler_params=_cparams("parallel", "parallel"),
        name="band_attention",
    )(pa, pa, pa, bias, pa)


def _conv_module_kernel(x_ref, halo_ref, w_ref, cb_ref, lg_ref, lb_ref, gate_ref, o_ref, xw_ref, y_ref):
    n_slabs, ts, slab = x_ref.shape
    c = n_slabs * slab
    i = pl.program_id(1)
    n_lane_blocks = c // V7X_LANES
    lbs_per_slab = slab // V7X_LANES

    def lanes_of(lb):
        return lb // lbs_per_slab, slice((lb % lbs_per_slab) * V7X_LANES, (lb % lbs_per_slab + 1) * V7X_LANES)

    for lb in range(n_lane_blocks):
        s, cols = lanes_of(lb)
        xw_ref[lb, CONV_HALO:CONV_HALO + ts, :] = x_ref[s, :, cols].astype(F32)

    @pl.when(i == 0)
    def _():
        xw_ref[:, 0:CONV_HALO, :] = jnp.zeros((n_lane_blocks, CONV_HALO, V7X_LANES), F32)

    @pl.when(i > 0)
    def _():
        for lb in range(n_lane_blocks):
            s, cols = lanes_of(lb)
            xw_ref[lb, 0:CONV_HALO, :] = halo_ref[s, :, cols].astype(F32)

    row_blk = 128
    sub = 8
    first_tap = CONV_HALO - (CONV_WIDTH - 1)

    def conv_body(it, carry):
        rb = it // n_lane_blocks
        lb = it % n_lane_blocks
        cols = pl.ds(pl.multiple_of(lb * V7X_LANES, V7X_LANES), V7X_LANES)
        base = rb * row_blk
        acc = [jnp.broadcast_to(cb_ref[:, cols], (sub, V7X_LANES))] * (row_blk // sub)
        for tap in range(CONV_WIDTH):
            wk = jnp.broadcast_to(w_ref[tap:tap + 1, cols], (sub, V7X_LANES))
            acc = [a + wk * xw_ref[lb, pl.ds(base + r * sub + first_tap + tap, sub), :]
                   for r, a in enumerate(acc)]
        for r, a in enumerate(acc):
            y_ref[pl.ds(pl.multiple_of(base + r * sub, sub), sub), cols] = a
        return carry

    lax.fori_loop(0, (ts // row_blk) * n_lane_blocks, conv_body, 0, unroll=2)

    ln_rows = 16

    def ln_body(r, carry):
        rows = pl.ds(pl.multiple_of(r * ln_rows, ln_rows), ln_rows)
        y = y_ref[rows, :]
        mu = jnp.mean(y, axis=-1, keepdims=True)
        d = y - mu
        var = jnp.mean(d * d, axis=-1, keepdims=True)
        z = (d * lax.rsqrt(var + EPS)) * lg_ref[...] + lb_ref[...]
        z = z * _sigmoid(z)
        gate = jnp.concatenate([gate_ref[s, rows, :] for s in range(n_slabs)], axis=1)
        o_ref[rows, :] = (z * gate.astype(F32)).astype(o_ref.dtype)
        return carry

    lax.fori_loop(0, ts // ln_rows, ln_body, 0, unroll=8)


def _conv_module(pa, conv_w, conv_b, ln_g, ln_b, *, batch, seq, w_b, x_col, gate_col, ts):
    slab = pa.shape[2]
    n_slabs = w_b // slab
    assert seq % ts == 0 and ts % 128 == 0 and ts % CONV_HALO == 0
    assert w_b % slab == 0 and x_col % w_b == 0 and gate_col % w_b == 0
    n_s = seq // ts
    halo_per_blk = ts // CONV_HALO
    return pl.pallas_call(
        _conv_module_kernel,
        grid=(batch, n_s),
        in_specs=[
            pl.BlockSpec((n_slabs, ts, slab), lambda b, i: (x_col // w_b, b * n_s + i, 0)),
            pl.BlockSpec((n_slabs, CONV_HALO, slab),
                         lambda b, i: (x_col // w_b, jnp.maximum((b * n_s + i) * halo_per_blk - 1, 0), 0)),
            pl.BlockSpec((CONV_WIDTH, w_b), lambda b, i: (0, 0)),
            pl.BlockSpec((1, w_b), lambda b, i: (0, 0)),
            pl.BlockSpec((1, w_b), lambda b, i: (0, 0)),
            pl.BlockSpec((1, w_b), lambda b, i: (0, 0)),
            pl.BlockSpec((n_slabs, ts, slab), lambda b, i: (gate_col // w_b, b * n_s + i, 0)),
        ],
        out_specs=pl.BlockSpec((ts, w_b), lambda b, i: (b * n_s + i, 0)),
        out_shape=jax.ShapeDtypeStruct((batch * seq, w_b), BF16),
        scratch_shapes=[pltpu.VMEM((w_b // V7X_LANES, CONV_HALO + ts, V7X_LANES), F32),
                        pltpu.VMEM((ts, w_b), F32)],
        compiler_params=_cparams("parallel", "arbitrary"),
        name="conv_module",
    )(pa, pa, conv_w.astype(F32), conv_b.reshape(1, w_b).astype(F32), ln_g.reshape(1, w_b).astype(F32),
      ln_b.reshape(1, w_b).astype(F32), pa)


def _sgu_kernel(ug_ref, v_ref, lg_ref, lb_ref, ws_ref, bs_ref, o_ref):
    tb, width = v_ref.shape
    gw = width // N_GROUPS_C
    pos_r = lax.broadcasted_iota(jnp.int32, (GMLP_CHUNK, GMLP_CHUNK), 0) // CHUNK
    pos_c = lax.broadcasted_iota(jnp.int32, (GMLP_CHUNK, GMLP_CHUNK), 1) // CHUNK
    causal = pos_r >= pos_c
    for n in range(tb // GMLP_CHUNK):
        rows = slice(n * GMLP_CHUNK, (n + 1) * GMLP_CHUNK)
        v = v_ref[rows, :].astype(F32)
        mu = jnp.mean(v, axis=-1, keepdims=True)
        d = v - mu
        var = jnp.mean(d * d, axis=-1, keepdims=True)
        vn = ((d * lax.rsqrt(var + EPS)) * lg_ref[...] + lb_ref[...]).astype(BF16)
        for g in range(N_GROUPS_C):
            cols = slice(g * gw, (g + 1) * gw)
            ws = jnp.where(causal, ws_ref[g], 0.0).astype(BF16)
            sg = _dot(ws, vn[:, cols]) + bs_ref[g]
            o_ref[rows, cols] = (ug_ref[rows, cols].astype(F32) * sg).astype(o_ref.dtype)


def _sgu(uv, ln_g, ln_b, w_s, b_s, *, width, tb):
    m = uv.shape[0]
    assert m % tb == 0 and tb % GMLP_CHUNK == 0
    return pl.pallas_call(
        _sgu_kernel,
        grid=(m // tb,),
        in_specs=[
            pl.BlockSpec((tb, width), lambda i: (i, 0)),
            pl.BlockSpec((tb, width), lambda i: (i, 1)),
            pl.BlockSpec((1, width), lambda i: (0, 0)),
            pl.BlockSpec((1, width), lambda i: (0, 0)),
            pl.BlockSpec((N_GROUPS_C, GMLP_CHUNK, GMLP_CHUNK), lambda i: (0, 0, 0)),
            pl.BlockSpec((N_GROUPS_C, GMLP_CHUNK, 1), lambda i: (0, 0, 0)),
        ],
        out_specs=pl.BlockSpec((tb, width), lambda i: (i, 0)),
        out_shape=jax.ShapeDtypeStruct((m, width), BF16),
        compiler_params=_cparams("parallel"),
        name="sgu",
    )(uv, uv, ln_g.reshape(1, width).astype(F32), ln_b.reshape(1, width).astype(F32),
      w_s.astype(F32), b_s.astype(F32)[..., None])


def _xattn_kernel(h_ref, g_ref, wq_ref, k_ref, v_ref, wo_ref, fg_ref, o_ref, hn_ref, q_ref, ctx_ref,
                  *, scale, final_norm):
    d = h_ref.shape[1]
    dh = d // N_HEADS_X
    x = h_ref[...]
    ms = jnp.mean(x * x, axis=-1, keepdims=True)
    hn_ref[...] = ((x * lax.rsqrt(ms + EPS)) * g_ref[...]).astype(BF16)
    q_ref[...] = (_dot(hn_ref[...], wq_ref[...]) * scale).astype(BF16)
    for h in range(N_HEADS_X):
        cols = slice(h * dh, (h + 1) * dh)
        s = _dot_nt(q_ref[:, cols], k_ref[:, cols])
        p = jnp.exp(s - jnp.max(s, axis=-1, keepdims=True))
        denom = jnp.sum(p, axis=-1, keepdims=True)
        ctx_ref[:, cols] = (_dot(p.astype(BF16), v_ref[:, cols]) / denom).astype(BF16)
    h_new = h_ref[...] + _dot(ctx_ref[...], wo_ref[...])
    if final_norm:
        ms = jnp.mean(h_new * h_new, axis=-1, keepdims=True)
        h_new = (h_new * lax.rsqrt(ms + EPS)) * fg_ref[...]
    o_ref[...] = h_new


def _cross_attention(h, g, wq, kv, wo, final_g, *, batch, seq, n_mem, tq, final_norm):
    m, d = h.shape
    assert seq % tq == 0
    n_s = seq // tq
    kern = functools.partial(_xattn_kernel, scale=(d // N_HEADS_X) ** -0.5, final_norm=final_norm)
    resident = dict(pipeline_mode=pl.Buffered(1))
    return pl.pallas_call(
        kern,
        grid=(batch, n_s),
        in_specs=[
            pl.BlockSpec((tq, d), lambda b, i: (b * n_s + i, 0)),
            pl.BlockSpec((1, d), lambda b, i: (0, 0)),
            pl.BlockSpec((d, d), lambda b, i: (0, 0), **resident),
            pl.BlockSpec((n_mem, d), lambda b, i: (b, 0)),
            pl.BlockSpec((n_mem, d), lambda b, i: (b, 1)),
            pl.BlockSpec((d, d), lambda b, i: (0, 0), **resident),
            pl.BlockSpec((1, d), lambda b, i: (0, 0)),
        ],
        out_specs=pl.BlockSpec((tq, d), lambda b, i: (b * n_s + i, 0)),
        out_shape=jax.ShapeDtypeStruct((m, d), F32),
        scratch_shapes=[pltpu.VMEM((tq, d), BF16), pltpu.VMEM((tq, d), BF16), pltpu.VMEM((tq, d), BF16)],
        compiler_params=_cparams("parallel", "arbitrary"),
        name="cross_attention",
    )(h, g.reshape(1, d).astype(F32), wq, kv, kv, wo, final_g.reshape(1, d).astype(F32))


def _pick(n, *cands):
    for c in cands:
        if n % c == 0:
            return c
    raise ValueError(f"no tile for {n} among {cands}")


def kernel(x, mem, norm_mix_g, norm_x_g, norm_mem_g, final_norm_g, w_in_ab, rel_bias, conv_w, conv_b,
           conv_ln_g, conv_ln_b, w_out_ab, w_in_c, sgu_ln_g, sgu_ln_b, w_s, b_s, w_out_c, w_xq, w_xk,
           w_xv, w_xo):
    batch, seq, d = x.shape
    n_mem = mem.shape[1]
    depth = norm_mix_g.shape[0]
    mix = w_out_ab.shape[1]
    w_a = mix // 2
    w_b = mix - w_a
    t = batch * seq
    tm = _pick(t, 1024, 512, 256)
    tn = _pick(w_a, 1024, 512, 256)
    nb_a, nb_b, nb_mix, nb_d = w_a // tn, w_b // tn, mix // tn, d // tn
    scale_a = HEAD_DIM_A ** -0.5

    h = x.reshape(t, d).astype(F32)
    mem2 = mem.reshape(batch * n_mem, d).astype(F32)
    for layer in range(depth):
        i = layer // 2
        if layer % 2 == 0:
            segs = [
                (nb_a, "plain", scale_a, 0, None),
                (2 * nb_a, "plain", 1.0, nb_a, None),
                (nb_b, "glu", 1.0, 3 * nb_a, 3 * nb_a + nb_b),
                (nb_mix, "silu", 1.0, 3 * nb_a + 2 * nb_b, None),
            ]
            pa = _norm_matmul(h, norm_mix_g[layer], w_in_ab[i].astype(BF16), segs, tm=tm, tn=tn, slab=ATT_SLAB)
            gate_col = 3 * w_a + w_b
            bias = _bias_tables(rel_bias[i])
            ya = _band_attention(pa, bias, batch=batch, seq=seq, w_a=w_a, gate_col=gate_col)
            yb = _conv_module(pa, conv_w[i], conv_b[i], conv_ln_g[i], conv_ln_b[i], batch=batch, seq=seq,
                              w_b=w_b, x_col=3 * w_a, gate_col=gate_col + w_a, ts=_pick(seq, 512, 256, 128))
            h = _resid_matmul(ya, 0, yb, 0, w_out_ab[i].astype(BF16), h, tm=tm, tn=tn)
        else:
            segs = [
                (nb_mix, "mulsilu", 1.0, 0, 2 * nb_mix),
                (nb_mix, "plain", 1.0, nb_mix, None),
            ]
            uv = _norm_matmul(h, norm_mix_g[layer], w_in_c[i].astype(BF16), segs, tm=tm, tn=tn)
            y = _sgu(uv, sgu_ln_g[i], sgu_ln_b[i], w_s[i], b_s[i], width=mix, tb=_pick(t, 512, 256, 128))
            h = _resid_matmul(y, 0, y, 1, w_out_c[i].astype(BF16), h, tm=tm, tn=tn)

        w_kv = jnp.concatenate([w_xk[layer], w_xv[layer]], axis=1).astype(BF16)
        kv = _norm_matmul(mem2, norm_mem_g[layer], w_kv, [(2 * nb_d, "plain", 1.0, 0, None)],
                          tm=_pick(batch * n_mem, 512, 256, 128), tn=tn)
        h = _cross_attention(h, norm_x_g[layer], w_xq[layer].astype(BF16), kv, w_xo[layer].astype(BF16),
                             final_norm_g, batch=batch, seq=seq, n_mem=n_mem, tq=_pick(seq, 512, 256, 128),
                             final_norm=(layer == depth - 1))
    return h.reshape(batch, seq, d)
```

```python
import functools

import jax
import jax.numpy as jnp
from jax import lax
from jax.experimental import pallas as pl
from jax.experimental.pallas import tpu as pltpu

F32 = jnp.float32
BF16 = jnp.bfloat16

CHUNK = 64
N_PAST_CHUNKS = 8
MAX_REL = 128
HEAD_DIM_A = 128
CONV_WIDTH = 31
GMLP_CHUNK = 128
N_GROUPS_C = 8
N_HEADS_X = 4
EPS = 1e-6
NEG_INF = -1e30

V7X_LANES = 128
V7X_VMEM_LIMIT_BYTES = 56 * 1024 * 1024

ATT_QB = 2 * CHUNK
ATT_WIN = N_PAST_CHUNKS * CHUNK + ATT_QB
ATT_ROLL_WIDTH = 768
ATT_N_VARIANTS = N_PAST_CHUNKS * CHUNK // ATT_QB + 1
REL_ROWS = 384
CONV_HALO = 32
ATT_SLAB = 2 * HEAD_DIM_A


def _cparams(*sem):
    return pltpu.CompilerParams(dimension_semantics=sem, vmem_limit_bytes=V7X_VMEM_LIMIT_BYTES)


def _sigmoid(x):
    return 0.5 * jnp.tanh(0.5 * x) + 0.5


def _dot(a, b):
    return jnp.dot(a, b, preferred_element_type=F32)


def _dot_nt(a, b):
    return lax.dot_general(a, b, (((1,), (1,)), ((), ())), preferred_element_type=F32)


def _norm_matmul_kernel(a_tab, b_tab, x_ref, g_ref, wa_ref, wb_ref, o_ref, hn_ref, *, segments):
    del a_tab, b_tab
    j = pl.program_id(1)
    rows_per_iter = 16

    @pl.when(j == 0)
    def _():
        g = g_ref[...]

        def body(r, carry):
            rows = pl.ds(pl.multiple_of(r * rows_per_iter, rows_per_iter), rows_per_iter)
            x = x_ref[rows, :].astype(F32)
            ms = jnp.mean(x * x, axis=-1, keepdims=True)
            hn_ref[rows, :] = ((x * lax.rsqrt(ms + EPS)) * g).astype(BF16)
            return carry

        lax.fori_loop(0, x_ref.shape[0] // rows_per_iter, body, 0, unroll=8)

    lo = 0
    for n_steps, kind, scale in segments:
        hi = lo + n_steps

        @pl.when((j >= lo) & (j < hi))
        def _(kind=kind, scale=scale):
            a = _dot(hn_ref[...], wa_ref[...])
            if kind == "plain":
                r = a if scale == 1.0 else a * scale
            elif kind == "silu":
                r = a * _sigmoid(a)
            elif kind == "glu":
                r = a * _sigmoid(_dot(hn_ref[...], wb_ref[...]))
            elif kind == "mulsilu":
                b = _dot(hn_ref[...], wb_ref[...])
                r = a * (b * _sigmoid(b))
            else:
                raise ValueError(kind)
            if len(o_ref.shape) == 2:
                o_ref[...] = r.astype(o_ref.dtype)
            else:
                slab = o_ref.shape[2]
                for c in range(o_ref.shape[0]):
                    o_ref[c] = r[:, c * slab:(c + 1) * slab].astype(o_ref.dtype)

        lo = hi


def _norm_matmul(x, g, w, segments, *, tm, tn, slab=None):
    m, k = x.shape
    assert m % tm == 0 and w.shape[0] == k and w.shape[1] % tn == 0
    a_tab, b_tab = [], []
    for n_steps, _, _, a0, b0 in segments:
        for s in range(n_steps):
            a_tab.append(a0 + s)
            b_tab.append(None if b0 is None else b0 + s)
    known = [b for b in b_tab if b is not None]
    fill = known[0] if known else 0
    for idx, b in enumerate(b_tab):
        if b is None:
            b_tab[idx] = fill
        else:
            fill = b
    n_steps_total = len(a_tab)
    kern = functools.partial(_norm_matmul_kernel, segments=tuple(s[:3] for s in segments))
    if slab is None:
        out_spec = pl.BlockSpec((tm, tn), lambda i, j, at, bt: (i, j))
        out_shape = jax.ShapeDtypeStruct((m, n_steps_total * tn), BF16)
    else:
        assert tn % slab == 0
        out_spec = pl.BlockSpec((tn // slab, tm, slab), lambda i, j, at, bt: (j, i, 0))
        out_shape = jax.ShapeDtypeStruct((n_steps_total * tn // slab, m, slab), BF16)
    grid_spec = pltpu.PrefetchScalarGridSpec(
        num_scalar_prefetch=2,
        grid=(m // tm, n_steps_total),
        in_specs=[
            pl.BlockSpec((tm, k), lambda i, j, at, bt: (i, 0)),
            pl.BlockSpec((1, k), lambda i, j, at, bt: (0, 0)),
            pl.BlockSpec((k, tn), lambda i, j, at, bt: (0, at[j])),
            pl.BlockSpec((k, tn), lambda i, j, at, bt: (0, bt[j])),
        ],
        out_specs=out_spec,
        scratch_shapes=[pltpu.VMEM((tm, k), BF16)],
    )
    return pl.pallas_call(
        kern,
        grid_spec=grid_spec,
        out_shape=out_shape,
        compiler_params=_cparams("parallel", "arbitrary"),
        name="norm_matmul",
    )(jnp.asarray(a_tab, jnp.int32), jnp.asarray(b_tab, jnp.int32), x, g.reshape(1, k), w, w)


def _resid_matmul_kernel(l0_ref, l1_ref, w0_ref, w1_ref, r_ref, o_ref):
    acc = _dot(l0_ref[...], w0_ref[...]) + _dot(l1_ref[...], w1_ref[...])
    o_ref[...] = r_ref[...] + acc


def _resid_matmul(l0, c0, l1, c1, w, resid, *, tm, tn):
    m, n = resid.shape
    kh = w.shape[0] // 2
    assert m % tm == 0 and n % tn == 0
    resident = dict(pipeline_mode=pl.Buffered(1)) if tn == n else {}
    return pl.pallas_call(
        _resid_matmul_kernel,
        grid=(m // tm, n // tn),
        in_specs=[
            pl.BlockSpec((tm, kh), lambda i, j: (i, c0)),
            pl.BlockSpec((tm, kh), lambda i, j: (i, c1)),
            pl.BlockSpec((kh, tn), lambda i, j: (0, j), **resident),
            pl.BlockSpec((kh, tn), lambda i, j: (1, j), **resident),
            pl.BlockSpec((tm, tn), lambda i, j: (i, j)),
        ],
        out_specs=pl.BlockSpec((tm, tn), lambda i, j: (i, j)),
        out_shape=jax.ShapeDtypeStruct((m, n), F32),
        compiler_params=_cparams("parallel", "arbitrary"),
        name="resid_matmul",
    )(l0, l1, w, w, resid)


def _bias_table_kernel(rb_ref, o_ref, *, n_heads):
    t = pl.program_id(0)
    width = ATT_ROLL_WIDTH
    m = lax.broadcasted_iota(jnp.int32, (REL_ROWS, width), 1)
    r = lax.broadcasted_iota(jnp.int32, (REL_ROWS, width), 0)
    mm = jnp.where(m < ATT_WIN, m, m - width)
    idx = jnp.clip(ATT_QB * t - mm, -MAX_REL, MAX_REL) + MAX_REL
    onehot = jnp.where(r == idx, 1.0, 0.0).astype(BF16)
    rb = rb_ref[...]
    hi = rb.astype(BF16)
    rem = rb - hi.astype(F32)
    mid = rem.astype(BF16)
    lo = (rem - mid.astype(F32)).astype(BF16)
    u = _dot(hi, onehot) + _dot(mid, onehot) + _dot(lo, onehot)

    qi = lax.broadcasted_iota(jnp.int32, (ATT_QB, ATT_WIN), 0) // CHUNK
    kj = lax.broadcasted_iota(jnp.int32, (ATT_QB, ATT_WIN), 1) // CHUNK
    q_chunk = (ATT_QB // CHUNK) * t + qi
    valid = (kj <= q_chunk) & (kj >= q_chunk - N_PAST_CHUNKS)
    for h in range(n_heads):
        x = jnp.broadcast_to(u[h:h + 1, :], (ATT_QB, width))
        toeplitz = pltpu.roll(x, 0, 1, stride=1, stride_axis=0)
        o_ref[0, h] = jnp.where(valid, toeplitz[:, :ATT_WIN], NEG_INF)


def _bias_tables(rel_bias):
    n_heads, n_rel = rel_bias.shape
    rb = jnp.pad(rel_bias.astype(F32), ((0, 0), (0, REL_ROWS - n_rel)))
    return pl.pallas_call(
        functools.partial(_bias_table_kernel, n_heads=n_heads),
        grid=(ATT_N_VARIANTS,),
        in_specs=[pl.BlockSpec((n_heads, REL_ROWS), lambda t: (0, 0))],
        out_specs=pl.BlockSpec((1, n_heads, ATT_QB, ATT_WIN), lambda t: (t, 0, 0, 0)),
        out_shape=jax.ShapeDtypeStruct((ATT_N_VARIANTS, n_heads, ATT_QB, ATT_WIN), F32),
        compiler_params=_cparams("arbitrary"),
        name="bias_tables",
    )(rb)


def _band_attn_kernel(q_ref, k_ref, v_ref, bias_ref, gate_ref, o_ref,
                      s_even, s_odd, p_even, p_odd, l_even, l_odd, *, heads_per_step):
    dh = HEAD_DIM_A
    n_blocks = q_ref.shape[0] // ATT_QB
    last_variant = ATT_N_VARIANTS - 1
    heads = [(h, slice(h * dh, (h + 1) * dh)) for h in range(heads_per_step)]

    def window(blk):
        return pl.ds(pl.multiple_of(jnp.maximum(blk - last_variant, 0) * ATT_QB, ATT_QB), ATT_WIN)

    def block_rows(blk):
        return pl.ds(pl.multiple_of(blk * ATT_QB, ATT_QB), ATT_QB)

    def scores(blk, s_buf):
        variant = jnp.minimum(blk, last_variant)
        for h, cols in heads:
            s_buf[h] = _dot_nt(q_ref[block_rows(blk), cols], k_ref[window(blk), cols]) + bias_ref[variant, h]

    def exponentiate(s_buf, p_buf, l_buf):
        for h, _ in heads:
            s = s_buf[h]
            p = jnp.exp(s - jnp.max(s, axis=-1, keepdims=True))
            l_buf[h] = jnp.sum(p, axis=-1, keepdims=True)
            p_buf[h] = p.astype(BF16)

    def apply_values(blk, p_buf, l_buf):
        rows = block_rows(blk)
        for h, cols in heads:
            o = _dot(p_buf[h], v_ref[window(blk), cols]) / l_buf[h]
            o_ref[rows, cols] = (o * gate_ref[rows, cols].astype(F32)).astype(o_ref.dtype)

    scores(0, s_even)
    scores(1, s_odd)
    exponentiate(s_even, p_even, l_even)

    def pair(t, carry):
        m = 2 * t + 1
        scores(m + 1, s_even)
        exponentiate(s_odd, p_odd, l_odd)
        apply_values(m - 1, p_even, l_even)
        scores(m + 2, s_odd)
        exponentiate(s_even, p_even, l_even)
        apply_values(m, p_odd, l_odd)
        return carry

    lax.fori_loop(0, (n_blocks - 2) // 2, pair, 0)
    exponentiate(s_odd, p_odd, l_odd)
    apply_values(n_blocks - 2, p_even, l_even)
    apply_values(n_blocks - 1, p_odd, l_odd)


def _band_attention(pa, bias, *, batch, seq, w_a, gate_col):
    heads_per_step = ATT_SLAB // HEAD_DIM_A
    cw = ATT_SLAB
    n_blocks = seq // ATT_QB
    assert seq % ATT_QB == 0 and n_blocks % 2 == 0 and n_blocks >= 4
    assert w_a % cw == 0 and gate_col % cw == 0 and pa.shape[2] == cw
    kern = functools.partial(_band_attn_kernel, heads_per_step=heads_per_step)
    seq_cols = lambda first: pl.BlockSpec((None, seq, cw), lambda b, g: (first // cw + g, b, 0))
    return pl.pallas_call(
        kern,
        grid=(batch, w_a // cw),
        in_specs=[
            seq_cols(0),
            seq_cols(w_a),
            seq_cols(2 * w_a),
            pl.BlockSpec((ATT_N_VARIANTS, heads_per_step, ATT_QB, ATT_WIN), lambda b, g: (0, g, 0, 0)),
            seq_cols(gate_col),
        ],
        out_specs=pl.BlockSpec((seq, cw), lambda b, g: (b, g)),
        out_shape=jax.ShapeDtypeStruct((batch * seq, w_a), BF16),
        scratch_shapes=[
            pltpu.VMEM((heads_per_step, ATT_QB, ATT_WIN), F32),
            pltpu.VMEM((heads_per_step, ATT_QB, ATT_WIN), F32),
            pltpu.VMEM((heads_per_step, ATT_QB, ATT_WIN), BF16),
            pltpu.VMEM((heads_per_step, ATT_QB, ATT_WIN), BF16),
            pltpu.VMEM((heads_per_step, ATT_QB, 1), F32),
            pltpu.VMEM((heads_per_step, ATT_QB, 1), F32),
        ],
        compiler_params=_cparams("parallel", "parallel"),
        name="band_attention",
    )(pa, pa, pa, bias, pa)


def _conv_module_kernel(x_ref, halo_ref, w_ref, cb_ref, lg_ref, lb_ref, gate_ref, o_ref, xw_ref, y_ref):
    n_slabs, ts, slab = x_ref.shape
    c = n_slabs * slab
    i = pl.program_id(1)
    n_lane_blocks = c // V7X_LANES
    lbs_per_slab = slab // V7X_LANES

    def lanes_of(lb):
        return lb // lbs_per_slab, slice((lb % lbs_per_slab) * V7X_LANES, (lb % lbs_per_slab + 1) * V7X_LANES)

    for lb in range(n_lane_blocks):
        s, cols = lanes_of(lb)
        xw_ref[lb, CONV_HALO:CONV_HALO + ts, :] = x_ref[s, :, cols].astype(F32)

    @pl.when(i == 0)
    def _():
        xw_ref[:, 0:CONV_HALO, :] = jnp.zeros((n_lane_blocks, CONV_HALO, V7X_LANES), F32)

    @pl.when(i > 0)
    def _():
        for lb in range(n_lane_blocks):
            s, cols = lanes_of(lb)
            xw_ref[lb, 0:CONV_HALO, :] = halo_ref[s, :, cols].astype(F32)

    row_blk = 128
    sub = 8
    first_tap = CONV_HALO - (CONV_WIDTH - 1)

    def conv_body(it, carry):
        rb = it // n_lane_blocks
        lb = it % n_lane_blocks
        cols = pl.ds(pl.multiple_of(lb * V7X_LANES, V7X_LANES), V7X_LANES)
        base = rb * row_blk
        acc = [jnp.broadcast_to(cb_ref[:, cols], (sub, V7X_LANES))] * (row_blk // sub)
        for tap in range(CONV_WIDTH):
            wk = jnp.broadcast_to(w_ref[tap:tap + 1, cols], (sub, V7X_LANES))
            acc = [a + wk * xw_ref[lb, pl.ds(base + r * sub + first_tap + tap, sub), :]
                   for r, a in enumerate(acc)]
        for r, a in enumerate(acc):
            y_ref[pl.ds(pl.multiple_of(base + r * sub, sub), sub), cols] = a
        return carry

    lax.fori_loop(0, (ts // row_blk) * n_lane_blocks, conv_body, 0, unroll=2)

    ln_rows = 16

    def ln_body(r, carry):
        rows = pl.ds(pl.multiple_of(r * ln_rows, ln_rows), ln_rows)
        y = y_ref[rows, :]
        mu = jnp.mean(y, axis=-1, keepdims=True)
        d = y - mu
        var = jnp.mean(d * d, axis=-1, keepdims=True)
        z = (d * lax.rsqrt(var + EPS)) * lg_ref[...] + lb_ref[...]
        z = z * _sigmoid(z)
        gate = jnp.concatenate([gate_ref[s, rows, :] for s in range(n_slabs)], axis=1)
        o_ref[rows, :] = (z * gate.astype(F32)).astype(o_ref.dtype)
        return carry

    lax.fori_loop(0, ts // ln_rows, ln_body, 0, unroll=8)


def _conv_module(pa, conv_w, conv_b, ln_g, ln_b, *, batch, seq, w_b, x_col, gate_col, ts):
    slab = pa.shape[2]
    n_slabs = w_b // slab
    assert seq % ts == 0 and ts % 128 == 0 and ts % CONV_HALO == 0
    assert w_b % slab == 0 and x_col % w_b == 0 and gate_col % w_b == 0
    n_s = seq // ts
    halo_per_blk = ts // CONV_HALO
    return pl.pallas_call(
        _conv_module_kernel,
        grid=(batch, n_s),
        in_specs=[
            pl.BlockSpec((n_slabs, ts, slab), lambda b, i: (x_col // w_b, b * n_s + i, 0)),
            pl.BlockSpec((n_slabs, CONV_HALO, slab),
                         lambda b, i: (x_col // w_b, jnp.maximum((b * n_s + i) * halo_per_blk - 1, 0), 0)),
            pl.BlockSpec((CONV_WIDTH, w_b), lambda b, i: (0, 0)),
            pl.BlockSpec((1, w_b), lambda b, i: (0, 0)),
            pl.BlockSpec((1, w_b), lambda b, i: (0, 0)),
            pl.BlockSpec((1, w_b), lambda b, i: (0, 0)),
            pl.BlockSpec((n_slabs, ts, slab), lambda b, i: (gate_col // w_b, b * n_s + i, 0)),
        ],
        out_specs=pl.BlockSpec((ts, w_b), lambda b, i: (b * n_s + i, 0)),
        out_shape=jax.ShapeDtypeStruct((batch * seq, w_b), BF16),
        scratch_shapes=[pltpu.VMEM((w_b // V7X_LANES, CONV_HALO + ts, V7X_LANES), F32),
                        pltpu.VMEM((ts, w_b), F32)],
        compiler_params=_cparams("parallel", "arbitrary"),
        name="conv_module",
    )(pa, pa, conv_w.astype(F32), conv_b.reshape(1, w_b).astype(F32), ln_g.reshape(1, w_b).astype(F32),
      ln_b.reshape(1, w_b).astype(F32), pa)


def _sgu_kernel(ug_ref, v_ref, lg_ref, lb_ref, ws_ref, bs_ref, o_ref):
    tb, width = v_ref.shape
    gw = width // N_GROUPS_C
    pos_r = lax.broadcasted_iota(jnp.int32, (GMLP_CHUNK, GMLP_CHUNK), 0) // CHUNK
    pos_c = lax.broadcasted_iota(jnp.int32, (GMLP_CHUNK, GMLP_CHUNK), 1) // CHUNK
    causal = pos_r >= pos_c
    for n in range(tb // GMLP_CHUNK):
        rows = slice(n * GMLP_CHUNK, (n + 1) * GMLP_CHUNK)
        v = v_ref[rows, :].astype(F32)
        mu = jnp.mean(v, axis=-1, keepdims=True)
        d = v - mu
        var = jnp.mean(d * d, axis=-1, keepdims=True)
        vn = ((d * lax.rsqrt(var + EPS)) * lg_ref[...] + lb_ref[...]).astype(BF16)
        for g in range(N_GROUPS_C):
            cols = slice(g * gw, (g + 1) * gw)
            ws = jnp.where(causal, ws_ref[g], 0.0).astype(BF16)
            sg = _dot(ws, vn[:, cols]) + bs_ref[g]
            o_ref[rows, cols] = (ug_ref[rows, cols].astype(F32) * sg).astype(o_ref.dtype)


def _sgu(uv, ln_g, ln_b, w_s, b_s, *, width, tb):
    m = uv.shape[0]
    assert m % tb == 0 and tb % GMLP_CHUNK == 0
    return pl.pallas_call(
        _sgu_kernel,
        grid=(m // tb,),
        in_specs=[
            pl.BlockSpec((tb, width), lambda i: (i, 0)),
            pl.BlockSpec((tb, width), lambda i: (i, 1)),
            pl.BlockSpec((1, width), lambda i: (0, 0)),
            pl.BlockSpec((1, width), lambda i: (0, 0)),
            pl.BlockSpec((N_GROUPS_C, GMLP_CHUNK, GMLP_CHUNK), lambda i: (0, 0, 0)),
            pl.BlockSpec((N_GROUPS_C, GMLP_CHUNK, 1), lambda i: (0, 0, 0)),
        ],
        out_specs=pl.BlockSpec((tb, width), lambda i: (i, 0)),
        out_shape=jax.ShapeDtypeStruct((m, width), BF16),
        compiler_params=_cparams("parallel"),
        name="sgu",
    )(uv, uv, ln_g.reshape(1, width).astype(F32), ln_b.reshape(1, width).astype(F32),
      w_s.astype(F32), b_s.astype(F32)[..., None])


def _xattn_kernel(h_ref, g_ref, wq_ref, k_ref, v_ref, wo_ref, fg_ref, o_ref, hn_ref, q_ref, ctx_ref,
                  *, scale, final_norm):
    d = h_ref.shape[1]
    dh = d // N_HEADS_X
    x = h_ref[...]
    ms = jnp.mean(x * x, axis=-1, keepdims=True)
    hn_ref[...] = ((x * lax.rsqrt(ms + EPS)) * g_ref[...]).astype(BF16)
    q_ref[...] = (_dot(hn_ref[...], wq_ref[...]) * scale).astype(BF16)
    for h in range(N_HEADS_X):
        cols = slice(h * dh, (h + 1) * dh)
        s = _dot_nt(q_ref[:, cols], k_ref[:, cols])
        p = jnp.exp(s - jnp.max(s, axis=-1, keepdims=True))
        denom = jnp.sum(p, axis=-1, keepdims=True)
        ctx_ref[:, cols] = (_dot(p.astype(BF16), v_ref[:, cols]) / denom).astype(BF16)
    h_new = h_ref[...] + _dot(ctx_ref[...], wo_ref[...])
    if final_norm:
        ms = jnp.mean(h_new * h_new, axis=-1, keepdims=True)
        h_new = (h_new * lax.rsqrt(ms + EPS)) * fg_ref[...]
    o_ref[...] = h_new


def _cross_attention(h, g, wq, kv, wo, final_g, *, batch, seq, n_mem, tq, final_norm):
    m, d = h.shape
    assert seq % tq == 0
    n_s = seq // tq
    kern = functools.partial(_xattn_kernel, scale=(d // N_HEADS_X) ** -0.5, final_norm=final_norm)
    resident = dict(pipeline_mode=pl.Buffered(1))
    return pl.pallas_call(
        kern,
        grid=(batch, n_s),
        in_specs=[
            pl.BlockSpec((tq, d), lambda b, i: (b * n_s + i, 0)),
            pl.BlockSpec((1, d), lambda b, i: (0, 0)),
            pl.BlockSpec((d, d), lambda b, i: (0, 0), **resident),
            pl.BlockSpec((n_mem, d), lambda b, i: (b, 0)),
            pl.BlockSpec((n_mem, d), lambda b, i: (b, 1)),
            pl.BlockSpec((d, d), lambda b, i: (0, 0), **resident),
            pl.BlockSpec((1, d), lambda b, i: (0, 0)),
        ],
        out_specs=pl.BlockSpec((tq, d), lambda b, i: (b * n_s + i, 0)),
        out_shape=jax.ShapeDtypeStruct((m, d), F32),
        scratch_shapes=[pltpu.VMEM((tq, d), BF16), pltpu.VMEM((tq, d), BF16), pltpu.VMEM((tq, d), BF16)],
        compiler_params=_cparams("parallel", "arbitrary"),
        name="cross_attention",
    )(h, g.reshape(1, d).astype(F32), wq, kv, kv, wo, final_g.reshape(1, d).astype(F32))


def _pick(n, *cands):
    for c in cands:
        if n % c == 0:
            return c
    raise ValueError(f"no tile for {n} among {cands}")


def kernel(x, mem, norm_mix_g, norm_x_g, norm_mem_g, final_norm_g, w_in_ab, rel_bias, conv_w, conv_b,
           conv_ln_g, conv_ln_b, w_out_ab, w_in_c, sgu_ln_g, sgu_ln_b, w_s, b_s, w_out_c, w_xq, w_xk,
           w_xv, w_xo):
    batch, seq, d = x.shape
    n_mem = mem.shape[1]
    depth = norm_mix_g.shape[0]
    mix = w_out_ab.shape[1]
    w_a = mix // 2
    w_b = mix - w_a
    t = batch * seq
    tm = _pick(t, 1024, 512, 256)
    tn = _pick(w_a, 1024, 512, 256)
    tm_out = _pick(t, 512, 256)
    nb_a, nb_b, nb_mix, nb_d = w_a // tn, w_b // tn, mix // tn, d // tn
    scale_a = HEAD_DIM_A ** -0.5

    h = x.reshape(t, d).astype(F32)
    mem2 = mem.reshape(batch * n_mem, d).astype(F32)
    for layer in range(depth):
        i = layer // 2
        if layer % 2 == 0:
            segs = [
                (nb_a, "plain", scale_a, 0, None),
                (2 * nb_a, "plain", 1.0, nb_a, None),
                (nb_b, "glu", 1.0, 3 * nb_a, 3 * nb_a + nb_b),
                (nb_mix, "silu", 1.0, 3 * nb_a + 2 * nb_b, None),
            ]
            pa = _norm_matmul(h, norm_mix_g[layer], w_in_ab[i].astype(BF16), segs, tm=tm, tn=tn, slab=ATT_SLAB)
            gate_col = 3 * w_a + w_b
            bias = _bias_tables(rel_bias[i])
            ya = _band_attention(pa, bias, batch=batch, seq=seq, w_a=w_a, gate_col=gate_col)
            yb = _conv_module(pa, conv_w[i], conv_b[i], conv_ln_g[i], conv_ln_b[i], batch=batch, seq=seq,
                              w_b=w_b, x_col=3 * w_a, gate_col=gate_col + w_a, ts=_pick(seq, 512, 256, 128))
            h = _resid_matmul(ya, 0, yb, 0, w_out_ab[i].astype(BF16), h, tm=tm_out, tn=d)
        else:
            segs = [
                (nb_mix, "mulsilu", 1.0, 0, 2 * nb_mix),
                (nb_mix, "plain", 1.0, nb_mix, None),
            ]
            uv = _norm_matmul(h, norm_mix_g[layer], w_in_c[i].astype(BF16), segs, tm=tm, tn=tn)
            y = _sgu(uv, sgu_ln_g[i], sgu_ln_b[i], w_s[i], b_s[i], width=mix, tb=_pick(t, 512, 256, 128))
            h = _resid_matmul(y, 0, y, 1, w_out_c[i].astype(BF16), h, tm=tm_out, tn=d)

        w_kv = jnp.concatenate([w_xk[layer], w_xv[layer]], axis=1).astype(BF16)
        kv = _norm_matmul(mem2, norm_mem_g[layer], w_kv, [(2 * nb_d, "plain", 1.0, 0, None)],
                          tm=_pick(batch * n_mem, 512, 256, 128), tn=tn)
        h = _cross_attention(h, norm_x_g[layer], w_xq[layer].astype(BF16), kv, w_xo[layer].astype(BF16),
                             final_norm_g, batch=batch, seq=seq, n_mem=n_mem, tq=_pick(seq, 512, 256, 128),
                             final_norm=(layer == depth - 1))
    return h.reshape(batch, seq, d)
```

```python
import functools

import jax
import jax.numpy as jnp
from jax import lax
from jax.experimental import pallas as pl
from jax.experimental.pallas import tpu as pltpu

F32 = jnp.float32
BF16 = jnp.bfloat16

CHUNK = 64
N_PAST_CHUNKS = 8
MAX_REL = 128
HEAD_DIM_A = 128
CONV_WIDTH = 31
GMLP_CHUNK = 128
N_GROUPS_C = 8
N_HEADS_X = 4
EPS = 1e-6
NEG_INF = -1e30

V7X_LANES = 128
V7X_VMEM_LIMIT_BYTES = 56 * 1024 * 1024

ATT_QB = 2 * CHUNK
ATT_WIN = N_PAST_CHUNKS * CHUNK + ATT_QB
ATT_ROLL_WIDTH = 768
ATT_N_VARIANTS = N_PAST_CHUNKS * CHUNK // ATT_QB + 1
REL_ROWS = 384
CONV_HALO = 32
ATT_SLAB = 2 * HEAD_DIM_A


def _cparams(*sem):
    return pltpu.CompilerParams(dimension_semantics=sem, vmem_limit_bytes=V7X_VMEM_LIMIT_BYTES)


def _sigmoid(x):
    return 0.5 * jnp.tanh(0.5 * x) + 0.5


def _dot(a, b):
    return jnp.dot(a, b, preferred_element_type=F32)


def _dot_nt(a, b):
    return lax.dot_general(a, b, (((1,), (1,)), ((), ())), preferred_element_type=F32)


def _norm_matmul_kernel(a_tab, b_tab, x_ref, g_ref, wa_ref, wb_ref, o_ref, hn_ref, *, segments):
    del a_tab, b_tab
    j = pl.program_id(1)
    rows_per_iter = 16

    @pl.when(j == 0)
    def _():
        g = g_ref[...]

        def body(r, carry):
            rows = pl.ds(pl.multiple_of(r * rows_per_iter, rows_per_iter), rows_per_iter)
            x = x_ref[rows, :].astype(F32)
            ms = jnp.mean(x * x, axis=-1, keepdims=True)
            hn_ref[rows, :] = ((x * lax.rsqrt(ms + EPS)) * g).astype(BF16)
            return carry

        lax.fori_loop(0, x_ref.shape[0] // rows_per_iter, body, 0, unroll=8)

    lo = 0
    for n_steps, kind, scale in segments:
        hi = lo + n_steps

        @pl.when((j >= lo) & (j < hi))
        def _(kind=kind, scale=scale):
            a = _dot(hn_ref[...], wa_ref[...])
            if kind == "plain":
                r = a if scale == 1.0 else a * scale
            elif kind == "silu":
                r = a * _sigmoid(a)
            elif kind == "glu":
                r = a * _sigmoid(_dot(hn_ref[...], wb_ref[...]))
            elif kind == "mulsilu":
                b = _dot(hn_ref[...], wb_ref[...])
                r = a * (b * _sigmoid(b))
            else:
                raise ValueError(kind)
            if len(o_ref.shape) == 2:
                o_ref[...] = r.astype(o_ref.dtype)
            else:
                slab = o_ref.shape[2]
                for c in range(o_ref.shape[0]):
                    o_ref[c] = r[:, c * slab:(c + 1) * slab].astype(o_ref.dtype)

        lo = hi


def _norm_matmul(x, g, w, segments, *, tm, tn, slab=None):
    m, k = x.shape
    assert m % tm == 0 and w.shape[0] == k and w.shape[1] % tn == 0
    a_tab, b_tab = [], []
    for n_steps, _, _, a0, b0 in segments:
        for s in range(n_steps):
            a_tab.append(a0 + s)
            b_tab.append(None if b0 is None else b0 + s)
    known = [b for b in b_tab if b is not None]
    fill = known[0] if known else 0
    for idx, b in enumerate(b_tab):
        if b is None:
            b_tab[idx] = fill
        else:
            fill = b
    n_steps_total = len(a_tab)
    kern = functools.partial(_norm_matmul_kernel, segments=tuple(s[:3] for s in segments))
    if slab is None:
        out_spec = pl.BlockSpec((tm, tn), lambda i, j, at, bt: (i, j))
        out_shape = jax.ShapeDtypeStruct((m, n_steps_total * tn), BF16)
    else:
        assert tn % slab == 0
        out_spec = pl.BlockSpec((tn // slab, tm, slab), lambda i, j, at, bt: (j, i, 0))
        out_shape = jax.ShapeDtypeStruct((n_steps_total * tn // slab, m, slab), BF16)
    grid_spec = pltpu.PrefetchScalarGridSpec(
        num_scalar_prefetch=2,
        grid=(m // tm, n_steps_total),
        in_specs=[
            pl.BlockSpec((tm, k), lambda i, j, at, bt: (i, 0)),
            pl.BlockSpec((1, k), lambda i, j, at, bt: (0, 0)),
            pl.BlockSpec((k, tn), lambda i, j, at, bt: (0, at[j])),
            pl.BlockSpec((k, tn), lambda i, j, at, bt: (0, bt[j])),
        ],
        out_specs=out_spec,
        scratch_shapes=[pltpu.VMEM((tm, k), BF16)],
    )
    return pl.pallas_call(
        kern,
        grid_spec=grid_spec,
        out_shape=out_shape,
        compiler_params=_cparams("parallel", "arbitrary"),
        name="norm_matmul",
    )(jnp.asarray(a_tab, jnp.int32), jnp.asarray(b_tab, jnp.int32), x, g.reshape(1, k), w, w)


def _resid_matmul_kernel(l0_ref, l1_ref, w0_ref, w1_ref, r_ref, o_ref):
    acc = _dot(l0_ref[...], w0_ref[...]) + _dot(l1_ref[...], w1_ref[...])
    o_ref[...] = r_ref[...] + acc


def _resid_matmul(l0, c0, l1, c1, w, resid, *, tm, tn):
    m, n = resid.shape
    kh = w.shape[0] // 2
    assert m % tm == 0 and n % tn == 0
    resident = dict(pipeline_mode=pl.Buffered(1)) if tn == n else {}
    return pl.pallas_call(
        _resid_matmul_kernel,
        grid=(m // tm, n // tn),
        in_specs=[
            pl.BlockSpec((tm, kh), lambda i, j: (i, c0)),
            pl.BlockSpec((tm, kh), lambda i, j: (i, c1)),
            pl.BlockSpec((kh, tn), lambda i, j: (0, j), **resident),
            pl.BlockSpec((kh, tn), lambda i, j: (1, j), **resident),
            pl.BlockSpec((tm, tn), lambda i, j: (i, j)),
        ],
        out_specs=pl.BlockSpec((tm, tn), lambda i, j: (i, j)),
        out_shape=jax.ShapeDtypeStruct((m, n), F32),
        compiler_params=_cparams("parallel", "arbitrary"),
        name="resid_matmul",
    )(l0, l1, w, w, resid)


def _bias_table_kernel(rb_ref, o_ref, *, n_heads):
    t = pl.program_id(0)
    width = ATT_ROLL_WIDTH
    m = lax.broadcasted_iota(jnp.int32, (REL_ROWS, width), 1)
    r = lax.broadcasted_iota(jnp.int32, (REL_ROWS, width), 0)
    mm = jnp.where(m < ATT_WIN, m, m - width)
    idx = jnp.clip(ATT_QB * t - mm, -MAX_REL, MAX_REL) + MAX_REL
    onehot = jnp.where(r == idx, 1.0, 0.0).astype(BF16)
    rb = rb_ref[...]
    hi = rb.astype(BF16)
    rem = rb - hi.astype(F32)
    mid = rem.astype(BF16)
    lo = (rem - mid.astype(F32)).astype(BF16)
    u = _dot(hi, onehot) + _dot(mid, onehot) + _dot(lo, onehot)

    qi = lax.broadcasted_iota(jnp.int32, (ATT_QB, ATT_WIN), 0) // CHUNK
    kj = lax.broadcasted_iota(jnp.int32, (ATT_QB, ATT_WIN), 1) // CHUNK
    q_chunk = (ATT_QB // CHUNK) * t + qi
    valid = (kj <= q_chunk) & (kj >= q_chunk - N_PAST_CHUNKS)
    for h in range(n_heads):
        x = jnp.broadcast_to(u[h:h + 1, :], (ATT_QB, width))
        toeplitz = pltpu.roll(x, 0, 1, stride=1, stride_axis=0)
        o_ref[0, h] = jnp.where(valid, toeplitz[:, :ATT_WIN], NEG_INF)


def _bias_tables(rel_bias):
    n_heads, n_rel = rel_bias.shape
    rb = jnp.pad(rel_bias.astype(F32), ((0, 0), (0, REL_ROWS - n_rel)))
    return pl.pallas_call(
        functools.partial(_bias_table_kernel, n_heads=n_heads),
        grid=(ATT_N_VARIANTS,),
        in_specs=[pl.BlockSpec((n_heads, REL_ROWS), lambda t: (0, 0))],
        out_specs=pl.BlockSpec((1, n_heads, ATT_QB, ATT_WIN), lambda t: (t, 0, 0, 0)),
        out_shape=jax.ShapeDtypeStruct((ATT_N_VARIANTS, n_heads, ATT_QB, ATT_WIN), F32),
        compiler_params=_cparams("arbitrary"),
        name="bias_tables",
    )(rb)


def _band_attn_kernel(q_ref, k_ref, v_ref, bias_ref, gate_ref, o_ref,
                      s_even, s_odd, p_even, p_odd, v1_ref, *, heads_per_step):
    dh = HEAD_DIM_A
    n_blocks = q_ref.shape[0] // ATT_QB
    last_variant = ATT_N_VARIANTS - 1
    heads = [(h, slice(h * dh, (h + 1) * dh)) for h in range(heads_per_step)]
    for h, cols in heads:
        v1_ref[h, :, 0:dh] = v_ref[:, cols]
        v1_ref[h, :, dh:2 * dh] = jnp.ones((v_ref.shape[0], dh), BF16)

    def window(blk):
        return pl.ds(pl.multiple_of(jnp.maximum(blk - last_variant, 0) * ATT_QB, ATT_QB), ATT_WIN)

    def block_rows(blk):
        return pl.ds(pl.multiple_of(blk * ATT_QB, ATT_QB), ATT_QB)

    def scores(blk, s_buf):
        variant = jnp.minimum(blk, last_variant)
        for h, cols in heads:
            s_buf[h] = _dot_nt(q_ref[block_rows(blk), cols], k_ref[window(blk), cols]) + bias_ref[variant, h]

    def exponentiate(s_buf, p_buf):
        for h, _ in heads:
            s = s_buf[h]
            p_buf[h] = jnp.exp((s - jnp.max(s, axis=-1, keepdims=True)).astype(BF16))

    def apply_values(blk, p_buf):
        rows = block_rows(blk)
        for h, cols in heads:
            ov = _dot(p_buf[h], v1_ref[h, window(blk), :])
            o = ov[:, 0:dh] / ov[:, dh:2 * dh]
            o_ref[rows, cols] = (o * gate_ref[rows, cols].astype(F32)).astype(o_ref.dtype)

    scores(0, s_even)
    scores(1, s_odd)
    exponentiate(s_even, p_even)

    def pair(t, carry):
        m = 2 * t + 1
        scores(m + 1, s_even)
        exponentiate(s_odd, p_odd)
        apply_values(m - 1, p_even)
        scores(m + 2, s_odd)
        exponentiate(s_even, p_even)
        apply_values(m, p_odd)
        return carry

    lax.fori_loop(0, (n_blocks - 2) // 2, pair, 0)
    exponentiate(s_odd, p_odd)
    apply_values(n_blocks - 2, p_even)
    apply_values(n_blocks - 1, p_odd)


def _band_attention(pa, bias, *, batch, seq, w_a, gate_col):
    heads_per_step = ATT_SLAB // HEAD_DIM_A
    cw = ATT_SLAB
    n_blocks = seq // ATT_QB
    assert seq % ATT_QB == 0 and n_blocks % 2 == 0 and n_blocks >= 4
    assert w_a % cw == 0 and gate_col % cw == 0 and pa.shape[2] == cw
    kern = functools.partial(_band_attn_kernel, heads_per_step=heads_per_step)
    seq_cols = lambda first: pl.BlockSpec((None, seq, cw), lambda b, g: (first // cw + g, b, 0))
    return pl.pallas_call(
        kern,
        grid=(batch, w_a // cw),
        in_specs=[
            seq_cols(0),
            seq_cols(w_a),
            pl.BlockSpec((None, seq, cw), lambda b, g: (2 * w_a // cw + g, b, 0), pipeline_mode=pl.Buffered(1)),
            pl.BlockSpec((ATT_N_VARIANTS, heads_per_step, ATT_QB, ATT_WIN), lambda b, g: (0, g, 0, 0)),
            seq_cols(gate_col),
        ],
        out_specs=pl.BlockSpec((seq, cw), lambda b, g: (b, g)),
        out_shape=jax.ShapeDtypeStruct((batch * seq, w_a), BF16),
        scratch_shapes=[
            pltpu.VMEM((heads_per_step, ATT_QB, ATT_WIN), F32),
            pltpu.VMEM((heads_per_step, ATT_QB, ATT_WIN), F32),
            pltpu.VMEM((heads_per_step, ATT_QB, ATT_WIN), BF16),
            pltpu.VMEM((heads_per_step, ATT_QB, ATT_WIN), BF16),
            pltpu.VMEM((heads_per_step, seq, 2 * HEAD_DIM_A), BF16),
        ],
        compiler_params=_cparams("parallel", "parallel"),
        name="band_attention",
    )(pa, pa, pa, bias, pa)


def _conv_module_kernel(x_ref, halo_ref, w_ref, cb_ref, lg_ref, lb_ref, gate_ref, o_ref, xw_ref, y_ref):
    n_slabs, ts, slab = x_ref.shape
    c = n_slabs * slab
    i = pl.program_id(1)
    n_lane_blocks = c // V7X_LANES
    lbs_per_slab = slab // V7X_LANES

    def lanes_of(lb):
        return lb // lbs_per_slab, slice((lb % lbs_per_slab) * V7X_LANES, (lb % lbs_per_slab + 1) * V7X_LANES)

    for lb in range(n_lane_blocks):
        s, cols = lanes_of(lb)
        xw_ref[lb, CONV_HALO:CONV_HALO + ts, :] = x_ref[s, :, cols].astype(F32)

    @pl.when(i == 0)
    def _():
        xw_ref[:, 0:CONV_HALO, :] = jnp.zeros((n_lane_blocks, CONV_HALO, V7X_LANES), F32)

    @pl.when(i > 0)
    def _():
        for lb in range(n_lane_blocks):
            s, cols = lanes_of(lb)
            xw_ref[lb, 0:CONV_HALO, :] = halo_ref[s, :, cols].astype(F32)

    row_blk = 128
    sub = 8
    first_tap = CONV_HALO - (CONV_WIDTH - 1)

    def conv_body(it, carry):
        rb = it // n_lane_blocks
        lb = it % n_lane_blocks
        cols = pl.ds(pl.multiple_of(lb * V7X_LANES, V7X_LANES), V7X_LANES)
        base = rb * row_blk
        acc = [jnp.broadcast_to(cb_ref[:, cols], (sub, V7X_LANES))] * (row_blk // sub)
        for tap in range(CONV_WIDTH):
            wk = jnp.broadcast_to(w_ref[tap:tap + 1, cols], (sub, V7X_LANES))
            acc = [a + wk * xw_ref[lb, pl.ds(base + r * sub + first_tap + tap, sub), :]
                   for r, a in enumerate(acc)]
        for r, a in enumerate(acc):
            y_ref[pl.ds(pl.multiple_of(base + r * sub, sub), sub), cols] = a
        return carry

    lax.fori_loop(0, (ts // row_blk) * n_lane_blocks, conv_body, 0, unroll=2)

    ln_rows = 16

    def ln_body(r, carry):
        rows = pl.ds(pl.multiple_of(r * ln_rows, ln_rows), ln_rows)
        y = y_ref[rows, :]
        mu = jnp.mean(y, axis=-1, keepdims=True)
        d = y - mu
        var = jnp.mean(d * d, axis=-1, keepdims=True)
        z = (d * lax.rsqrt(var + EPS)) * lg_ref[...] + lb_ref[...]
        z = z * _sigmoid(z)
        gate = jnp.concatenate([gate_ref[s, rows, :] for s in range(n_slabs)], axis=1)
        o_ref[rows, :] = (z * gate.astype(F32)).astype(o_ref.dtype)
        return carry

    lax.fori_loop(0, ts // ln_rows, ln_body, 0, unroll=8)


def _conv_module(pa, conv_w, conv_b, ln_g, ln_b, *, batch, seq, w_b, x_col, gate_col, ts):
    slab = pa.shape[2]
    n_slabs = w_b // slab
    assert seq % ts == 0 and ts % 128 == 0 and ts % CONV_HALO == 0
    assert w_b % slab == 0 and x_col % w_b == 0 and gate_col % w_b == 0
    n_s = seq // ts
    halo_per_blk = ts // CONV_HALO
    return pl.pallas_call(
        _conv_module_kernel,
        grid=(batch, n_s),
        in_specs=[
            pl.BlockSpec((n_slabs, ts, slab), lambda b, i: (x_col // w_b, b * n_s + i, 0)),
            pl.BlockSpec((n_slabs, CONV_HALO, slab),
                         lambda b, i: (x_col // w_b, jnp.maximum((b * n_s + i) * halo_per_blk - 1, 0), 0)),
            pl.BlockSpec((CONV_WIDTH, w_b), lambda b, i: (0, 0)),
            pl.BlockSpec((1, w_b), lambda b, i: (0, 0)),
            pl.BlockSpec((1, w_b), lambda b, i: (0, 0)),
            pl.BlockSpec((1, w_b), lambda b, i: (0, 0)),
            pl.BlockSpec((n_slabs, ts, slab), lambda b, i: (gate_col // w_b, b * n_s + i, 0)),
        ],
        out_specs=pl.BlockSpec((ts, w_b), lambda b, i: (b * n_s + i, 0)),
        out_shape=jax.ShapeDtypeStruct((batch * seq, w_b), BF16),
        scratch_shapes=[pltpu.VMEM((w_b // V7X_LANES, CONV_HALO + ts, V7X_LANES), F32),
                        pltpu.VMEM((ts, w_b), F32)],
        compiler_params=_cparams("parallel", "arbitrary"),
        name="conv_module",
    )(pa, pa, conv_w.astype(F32), conv_b.reshape(1, w_b).astype(F32), ln_g.reshape(1, w_b).astype(F32),
      ln_b.reshape(1, w_b).astype(F32), pa)


def _sgu_kernel(ug_ref, v_ref, lg_ref, lb_ref, ws_ref, bs_ref, o_ref):
    tb, width = v_ref.shape
    gw = width // N_GROUPS_C
    pos_r = lax.broadcasted_iota(jnp.int32, (GMLP_CHUNK, GMLP_CHUNK), 0) // CHUNK
    pos_c = lax.broadcasted_iota(jnp.int32, (GMLP_CHUNK, GMLP_CHUNK), 1) // CHUNK
    causal = pos_r >= pos_c
    for n in range(tb // GMLP_CHUNK):
        rows = slice(n * GMLP_CHUNK, (n + 1) * GMLP_CHUNK)
        v = v_ref[rows, :].astype(F32)
        mu = jnp.mean(v, axis=-1, keepdims=True)
        d = v - mu
        var = jnp.mean(d * d, axis=-1, keepdims=True)
        vn = ((d * lax.rsqrt(var + EPS)) * lg_ref[...] + lb_ref[...]).astype(BF16)
        for g in range(N_GROUPS_C):
            cols = slice(g * gw, (g + 1) * gw)
            ws = jnp.where(causal, ws_ref[g], 0.0).astype(BF16)
            sg = _dot(ws, vn[:, cols]) + bs_ref[g]
            o_ref[rows, cols] = (ug_ref[rows, cols].astype(F32) * sg).astype(o_ref.dtype)


def _sgu(uv, ln_g, ln_b, w_s, b_s, *, width, tb):
    m = uv.shape[0]
    assert m % tb == 0 and tb % GMLP_CHUNK == 0
    return pl.pallas_call(
        _sgu_kernel,
        grid=(m // tb,),
        in_specs=[
            pl.BlockSpec((tb, width), lambda i: (i, 0)),
            pl.BlockSpec((tb, width), lambda i: (i, 1)),
            pl.BlockSpec((1, width), lambda i: (0, 0)),
            pl.BlockSpec((1, width), lambda i: (0, 0)),
            pl.BlockSpec((N_GROUPS_C, GMLP_CHUNK, GMLP_CHUNK), lambda i: (0, 0, 0)),
            pl.BlockSpec((N_GROUPS_C, GMLP_CHUNK, 1), lambda i: (0, 0, 0)),
        ],
        out_specs=pl.BlockSpec((tb, width), lambda i: (i, 0)),
        out_shape=jax.ShapeDtypeStruct((m, width), BF16),
        compiler_params=_cparams("parallel"),
        name="sgu",
    )(uv, uv, ln_g.reshape(1, width).astype(F32), ln_b.reshape(1, width).astype(F32),
      w_s.astype(F32), b_s.astype(F32)[..., None])


def _xattn_kernel(h_ref, g_ref, wq_ref, k_ref, v_ref, wo_ref, fg_ref, o_ref, hn_ref, q_ref, ctx_ref,
                  *, scale, final_norm):
    d = h_ref.shape[1]
    dh = d // N_HEADS_X
    x = h_ref[...]
    ms = jnp.mean(x * x, axis=-1, keepdims=True)
    hn_ref[...] = ((x * lax.rsqrt(ms + EPS)) * g_ref[...]).astype(BF16)
    q_ref[...] = (_dot(hn_ref[...], wq_ref[...]) * scale).astype(BF16)
    for h in range(N_HEADS_X):
        cols = slice(h * dh, (h + 1) * dh)
        s = _dot_nt(q_ref[:, cols], k_ref[:, cols])
        p = jnp.exp(s - jnp.max(s, axis=-1, keepdims=True))
        denom = jnp.sum(p, axis=-1, keepdims=True)
        ctx_ref[:, cols] = (_dot(p.astype(BF16), v_ref[:, cols]) / denom).astype(BF16)
    h_new = h_ref[...] + _dot(ctx_ref[...], wo_ref[...])
    if final_norm:
        ms = jnp.mean(h_new * h_new, axis=-1, keepdims=True)
        h_new = (h_new * lax.rsqrt(ms + EPS)) * fg_ref[...]
    o_ref[...] = h_new


def _cross_attention(h, g, wq, kv, wo, final_g, *, batch, seq, n_mem, tq, final_norm):
    m, d = h.shape
    assert seq % tq == 0
    n_s = seq // tq
    kern = functools.partial(_xattn_kernel, scale=(d // N_HEADS_X) ** -0.5, final_norm=final_norm)
    resident = dict(pipeline_mode=pl.Buffered(1))
    return pl.pallas_call(
        kern,
        grid=(batch, n_s),
        in_specs=[
            pl.BlockSpec((tq, d), lambda b, i: (b * n_s + i, 0)),
            pl.BlockSpec((1, d), lambda b, i: (0, 0)),
            pl.BlockSpec((d, d), lambda b, i: (0, 0), **resident),
            pl.BlockSpec((n_mem, d), lambda b, i: (b, 0)),
            pl.BlockSpec((n_mem, d), lambda b, i: (b, 1)),
            pl.BlockSpec((d, d), lambda b, i: (0, 0), **resident),
            pl.BlockSpec((1, d), lambda b, i: (0, 0)),
        ],
        out_specs=pl.BlockSpec((tq, d), lambda b, i: (b * n_s + i, 0)),
        out_shape=jax.ShapeDtypeStruct((m, d), F32),
        scratch_shapes=[pltpu.VMEM((tq, d), BF16), pltpu.VMEM((tq, d), BF16), pltpu.VMEM((tq, d), BF16)],
        compiler_params=_cparams("parallel", "arbitrary"),
        name="cross_attention",
    )(h, g.reshape(1, d).astype(F32), wq, kv, kv, wo, final_g.reshape(1, d).astype(F32))


def _pick(n, *cands):
    for c in cands:
        if n % c == 0:
            return c
    raise ValueError(f"no tile for {n} among {cands}")


def kernel(x, mem, norm_mix_g, norm_x_g, norm_mem_g, final_norm_g, w_in_ab, rel_bias, conv_w, conv_b,
           conv_ln_g, conv_ln_b, w_out_ab, w_in_c, sgu_ln_g, sgu_ln_b, w_s, b_s, w_out_c, w_xq, w_xk,
           w_xv, w_xo):
    batch, seq, d = x.shape
    n_mem = mem.shape[1]
    depth = norm_mix_g.shape[0]
    mix = w_out_ab.shape[1]
    w_a = mix // 2
    w_b = mix - w_a
    t = batch * seq
    tm = _pick(t, 1024, 512, 256)
    tn = _pick(w_a, 1024, 512, 256)
    tm_out = _pick(t, 512, 256)
    nb_a, nb_b, nb_mix, nb_d = w_a // tn, w_b // tn, mix // tn, d // tn
    scale_a = HEAD_DIM_A ** -0.5

    h = x.reshape(t, d).astype(F32)
    mem2 = mem.reshape(batch * n_mem, d).astype(F32)
    for layer in range(depth):
        i = layer // 2
        if layer % 2 == 0:
            segs = [
                (nb_a, "plain", scale_a, 0, None),
                (2 * nb_a, "plain", 1.0, nb_a, None),
                (nb_b, "glu", 1.0, 3 * nb_a, 3 * nb_a + nb_b),
                (nb_mix, "silu", 1.0, 3 * nb_a + 2 * nb_b, None),
            ]
            pa = _norm_matmul(h, norm_mix_g[layer], w_in_ab[i].astype(BF16), segs, tm=tm, tn=tn, slab=ATT_SLAB)
            gate_col = 3 * w_a + w_b
            bias = _bias_tables(rel_bias[i])
            ya = _band_attention(pa, bias, batch=batch, seq=seq, w_a=w_a, gate_col=gate_col)
            yb = _conv_module(pa, conv_w[i], conv_b[i], conv_ln_g[i], conv_ln_b[i], batch=batch, seq=seq,
                              w_b=w_b, x_col=3 * w_a, gate_col=gate_col + w_a, ts=_pick(seq, 512, 256, 128))
            h = _resid_matmul(ya, 0, yb, 0, w_out_ab[i].astype(BF16), h, tm=tm_out, tn=d)
        else:
            segs = [
                (nb_mix, "mulsilu", 1.0, 0, 2 * nb_mix),
                (nb_mix, "plain", 1.0, nb_mix, None),
            ]
            uv = _norm_matmul(h, norm_mix_g[layer], w_in_c[i].astype(BF16), segs, tm=tm, tn=tn)
            y = _sgu(uv, sgu_ln_g[i], sgu_ln_b[i], w_s[i], b_s[i], width=mix, tb=_pick(t, 512, 256, 128))
            h = _resid_matmul(y, 0, y, 1, w_out_c[i].astype(BF16), h, tm=tm_out, tn=d)

        w_kv = jnp.concatenate([w_xk[layer], w_xv[layer]], axis=1).astype(BF16)
        kv = _norm_matmul(mem2, norm_mem_g[layer], w_kv, [(2 * nb_d, "plain", 1.0, 0, None)],
                          tm=_pick(batch * n_mem, 512, 256, 128), tn=tn)
        h = _cross_attention(h, norm_x_g[layer], w_xq[layer].astype(BF16), kv, w_xo[layer].astype(BF16),
                             final_norm_g, batch=batch, seq=seq, n_mem=n_mem, tq=_pick(seq, 512, 256, 128),
                             final_norm=(layer == depth - 1))
    return h.reshape(batch, seq, d)
```

```python
import functools

import jax
import jax.numpy as jnp
from jax import lax
from jax.experimental import pallas as pl
from jax.experimental.pallas import tpu as pltpu

F32 = jnp.float32
BF16 = jnp.bfloat16

CHUNK = 64
N_PAST_CHUNKS = 8
MAX_REL = 128
HEAD_DIM_A = 128
CONV_WIDTH = 31
GMLP_CHUNK = 128
N_GROUPS_C = 8
N_HEADS_X = 4
EPS = 1e-6
NEG_INF = -1e30

V7X_LANES = 128
V7X_VMEM_LIMIT_BYTES = 56 * 1024 * 1024

ATT_QB = 4 * CHUNK
ATT_WIN = N_PAST_CHUNKS * CHUNK + ATT_QB
ATT_ROLL_WIDTH = 1024
ATT_N_VARIANTS = N_PAST_CHUNKS * CHUNK // ATT_QB + 1
REL_ROWS = 384
CONV_HALO = 32
ATT_SLAB = 2 * HEAD_DIM_A


def _cparams(*sem):
    return pltpu.CompilerParams(dimension_semantics=sem, vmem_limit_bytes=V7X_VMEM_LIMIT_BYTES)


def _sigmoid(x):
    return 0.5 * jnp.tanh(0.5 * x) + 0.5


def _dot(a, b):
    return jnp.dot(a, b, preferred_element_type=F32)


def _dot_nt(a, b):
    return lax.dot_general(a, b, (((1,), (1,)), ((), ())), preferred_element_type=F32)


def _norm_matmul_kernel(a_tab, b_tab, x_ref, g_ref, wa_ref, wb_ref, o_ref, hn_ref, *, segments):
    del a_tab, b_tab
    j = pl.program_id(1)
    rows_per_iter = 16

    @pl.when(j == 0)
    def _():
        g = g_ref[...]

        def body(r, carry):
            rows = pl.ds(pl.multiple_of(r * rows_per_iter, rows_per_iter), rows_per_iter)
            x = x_ref[rows, :].astype(F32)
            ms = jnp.mean(x * x, axis=-1, keepdims=True)
            hn_ref[rows, :] = ((x * lax.rsqrt(ms + EPS)) * g).astype(BF16)
            return carry

        lax.fori_loop(0, x_ref.shape[0] // rows_per_iter, body, 0, unroll=8)

    lo = 0
    for n_steps, kind, scale in segments:
        hi = lo + n_steps

        @pl.when((j >= lo) & (j < hi))
        def _(kind=kind, scale=scale):
            a = _dot(hn_ref[...], wa_ref[...])
            if kind == "plain":
                r = a if scale == 1.0 else a * scale
            elif kind == "silu":
                r = a * _sigmoid(a)
            elif kind == "glu":
                r = a * _sigmoid(_dot(hn_ref[...], wb_ref[...]))
            elif kind == "mulsilu":
                b = _dot(hn_ref[...], wb_ref[...])
                r = a * (b * _sigmoid(b))
            else:
                raise ValueError(kind)
            if len(o_ref.shape) == 2:
                o_ref[...] = r.astype(o_ref.dtype)
            else:
                slab = o_ref.shape[2]
                for c in range(o_ref.shape[0]):
                    o_ref[c] = r[:, c * slab:(c + 1) * slab].astype(o_ref.dtype)

        lo = hi


def _norm_matmul(x, g, w, segments, *, tm, tn, slab=None):
    m, k = x.shape
    assert m % tm == 0 and w.shape[0] == k and w.shape[1] % tn == 0
    a_tab, b_tab = [], []
    for n_steps, _, _, a0, b0 in segments:
        for s in range(n_steps):
            a_tab.append(a0 + s)
            b_tab.append(None if b0 is None else b0 + s)
    known = [b for b in b_tab if b is not None]
    fill = known[0] if known else 0
    for idx, b in enumerate(b_tab):
        if b is None:
            b_tab[idx] = fill
        else:
            fill = b
    n_steps_total = len(a_tab)
    kern = functools.partial(_norm_matmul_kernel, segments=tuple(s[:3] for s in segments))
    if slab is None:
        out_spec = pl.BlockSpec((tm, tn), lambda i, j, at, bt: (i, j))
        out_shape = jax.ShapeDtypeStruct((m, n_steps_total * tn), BF16)
    else:
        assert tn % slab == 0
        out_spec = pl.BlockSpec((tn // slab, tm, slab), lambda i, j, at, bt: (j, i, 0))
        out_shape = jax.ShapeDtypeStruct((n_steps_total * tn // slab, m, slab), BF16)
    grid_spec = pltpu.PrefetchScalarGridSpec(
        num_scalar_prefetch=2,
        grid=(m // tm, n_steps_total),
        in_specs=[
            pl.BlockSpec((tm, k), lambda i, j, at, bt: (i, 0)),
            pl.BlockSpec((1, k), lambda i, j, at, bt: (0, 0)),
            pl.BlockSpec((k, tn), lambda i, j, at, bt: (0, at[j])),
            pl.BlockSpec((k, tn), lambda i, j, at, bt: (0, bt[j])),
        ],
        out_specs=out_spec,
        scratch_shapes=[pltpu.VMEM((tm, k), BF16)],
    )
    return pl.pallas_call(
        kern,
        grid_spec=grid_spec,
        out_shape=out_shape,
        compiler_params=_cparams("parallel", "arbitrary"),
        name="norm_matmul",
    )(jnp.asarray(a_tab, jnp.int32), jnp.asarray(b_tab, jnp.int32), x, g.reshape(1, k), w, w)


def _resid_matmul_kernel(l0_ref, l1_ref, w0_ref, w1_ref, r_ref, o_ref):
    acc = _dot(l0_ref[...], w0_ref[...]) + _dot(l1_ref[...], w1_ref[...])
    o_ref[...] = r_ref[...] + acc


def _resid_matmul(l0, c0, l1, c1, w, resid, *, tm, tn):
    m, n = resid.shape
    kh = w.shape[0] // 2
    assert m % tm == 0 and n % tn == 0
    resident = dict(pipeline_mode=pl.Buffered(1)) if tn == n else {}
    return pl.pallas_call(
        _resid_matmul_kernel,
        grid=(m // tm, n // tn),
        in_specs=[
            pl.BlockSpec((tm, kh), lambda i, j: (i, c0)),
            pl.BlockSpec((tm, kh), lambda i, j: (i, c1)),
            pl.BlockSpec((kh, tn), lambda i, j: (0, j), **resident),
            pl.BlockSpec((kh, tn), lambda i, j: (1, j), **resident),
            pl.BlockSpec((tm, tn), lambda i, j: (i, j)),
        ],
        out_specs=pl.BlockSpec((tm, tn), lambda i, j: (i, j)),
        out_shape=jax.ShapeDtypeStruct((m, n), F32),
        compiler_params=_cparams("parallel", "arbitrary"),
        name="resid_matmul",
    )(l0, l1, w, w, resid)


def _bias_table_kernel(rb_ref, o_ref, *, n_heads):
    t = pl.program_id(0)
    width = ATT_ROLL_WIDTH
    m = lax.broadcasted_iota(jnp.int32, (REL_ROWS, width), 1)
    r = lax.broadcasted_iota(jnp.int32, (REL_ROWS, width), 0)
    mm = jnp.where(m < ATT_WIN, m, m - width)
    idx = jnp.clip(ATT_QB * t - mm, -MAX_REL, MAX_REL) + MAX_REL
    onehot = jnp.where(r == idx, 1.0, 0.0).astype(BF16)
    rb = rb_ref[...]
    hi = rb.astype(BF16)
    rem = rb - hi.astype(F32)
    mid = rem.astype(BF16)
    lo = (rem - mid.astype(F32)).astype(BF16)
    u = _dot(hi, onehot) + _dot(mid, onehot) + _dot(lo, onehot)

    qi = lax.broadcasted_iota(jnp.int32, (ATT_QB, ATT_WIN), 0) // CHUNK
    kj = lax.broadcasted_iota(jnp.int32, (ATT_QB, ATT_WIN), 1) // CHUNK
    q_chunk = (ATT_QB // CHUNK) * t + qi
    valid = (kj <= q_chunk) & (kj >= q_chunk - N_PAST_CHUNKS)
    for h in range(n_heads):
        x = jnp.broadcast_to(u[h:h + 1, :], (ATT_QB, width))
        toeplitz = pltpu.roll(x, 0, 1, stride=1, stride_axis=0)
        o_ref[0, h] = jnp.where(valid, toeplitz[:, :ATT_WIN], NEG_INF)


def _bias_tables(rel_bias):
    n_heads, n_rel = rel_bias.shape
    rb = jnp.pad(rel_bias.astype(F32), ((0, 0), (0, REL_ROWS - n_rel)))
    return pl.pallas_call(
        functools.partial(_bias_table_kernel, n_heads=n_heads),
        grid=(ATT_N_VARIANTS,),
        in_specs=[pl.BlockSpec((n_heads, REL_ROWS), lambda t: (0, 0))],
        out_specs=pl.BlockSpec((1, n_heads, ATT_QB, ATT_WIN), lambda t: (t, 0, 0, 0)),
        out_shape=jax.ShapeDtypeStruct((ATT_N_VARIANTS, n_heads, ATT_QB, ATT_WIN), F32),
        compiler_params=_cparams("arbitrary"),
        name="bias_tables",
    )(rb)


def _band_attn_kernel(q_ref, k_ref, v_ref, bias_ref, gate_ref, o_ref,
                      s_even, s_odd, p_even, p_odd, l_even, l_odd, *, heads_per_step):
    dh = HEAD_DIM_A
    n_blocks = q_ref.shape[0] // ATT_QB
    last_variant = ATT_N_VARIANTS - 1
    heads = [(h, slice(h * dh, (h + 1) * dh)) for h in range(heads_per_step)]

    def window(blk):
        return pl.ds(pl.multiple_of(jnp.maximum(blk - last_variant, 0) * ATT_QB, ATT_QB), ATT_WIN)

    def block_rows(blk):
        return pl.ds(pl.multiple_of(blk * ATT_QB, ATT_QB), ATT_QB)

    def scores(blk, s_buf):
        variant = jnp.minimum(blk, last_variant)
        for h, cols in heads:
            s_buf[h] = _dot_nt(q_ref[block_rows(blk), cols], k_ref[window(blk), cols]) + bias_ref[variant, h]

    def exponentiate(s_buf, p_buf, l_buf):
        for h, _ in heads:
            s = s_buf[h]
            p = jnp.exp(s - jnp.max(s, axis=-1, keepdims=True))
            l_buf[h] = jnp.sum(p, axis=-1, keepdims=True)
            p_buf[h] = p.astype(BF16)

    def apply_values(blk, p_buf, l_buf):
        rows = block_rows(blk)
        for h, cols in heads:
            o = _dot(p_buf[h], v_ref[window(blk), cols]) / l_buf[h]
            o_ref[rows, cols] = (o * gate_ref[rows, cols].astype(F32)).astype(o_ref.dtype)

    scores(0, s_even)
    scores(1, s_odd)
    exponentiate(s_even, p_even, l_even)

    def pair(t, carry):
        m = 2 * t + 1
        scores(m + 1, s_even)
        exponentiate(s_odd, p_odd, l_odd)
        apply_values(m - 1, p_even, l_even)
        scores(m + 2, s_odd)
        exponentiate(s_even, p_even, l_even)
        apply_values(m, p_odd, l_odd)
        return carry

    lax.fori_loop(0, (n_blocks - 2) // 2, pair, 0)
    exponentiate(s_odd, p_odd, l_odd)
    apply_values(n_blocks - 2, p_even, l_even)
    apply_values(n_blocks - 1, p_odd, l_odd)


def _band_attention(pa, bias, *, batch, seq, w_a, gate_col):
    heads_per_step = ATT_SLAB // HEAD_DIM_A
    cw = ATT_SLAB
    n_blocks = seq // ATT_QB
    assert seq % ATT_QB == 0 and n_blocks % 2 == 0 and n_blocks >= 4
    assert w_a % cw == 0 and gate_col % cw == 0 and pa.shape[2] == cw
    kern = functools.partial(_band_attn_kernel, heads_per_step=heads_per_step)
    seq_cols = lambda first: pl.BlockSpec((None, seq, cw), lambda b, g: (first // cw + g, b, 0))
    return pl.pallas_call(
        kern,
        grid=(batch, w_a // cw),
        in_specs=[
            seq_cols(0),
            seq_cols(w_a),
            seq_cols(2 * w_a),
            pl.BlockSpec((ATT_N_VARIANTS, heads_per_step, ATT_QB, ATT_WIN), lambda b, g: (0, g, 0, 0),
                         pipeline_mode=pl.Buffered(1)),
            seq_cols(gate_col),
        ],
        out_specs=pl.BlockSpec((seq, cw), lambda b, g: (b, g)),
        out_shape=jax.ShapeDtypeStruct((batch * seq, w_a), BF16),
        scratch_shapes=[
            pltpu.VMEM((heads_per_step, ATT_QB, ATT_WIN), F32),
            pltpu.VMEM((heads_per_step, ATT_QB, ATT_WIN), F32),
            pltpu.VMEM((heads_per_step, ATT_QB, ATT_WIN), BF16),
            pltpu.VMEM((heads_per_step, ATT_QB, ATT_WIN), BF16),
            pltpu.VMEM((heads_per_step, ATT_QB, 1), F32),
            pltpu.VMEM((heads_per_step, ATT_QB, 1), F32),
        ],
        compiler_params=_cparams("parallel", "parallel"),
        name="band_attention",
    )(pa, pa, pa, bias, pa)


def _conv_module_kernel(x_ref, halo_ref, w_ref, cb_ref, lg_ref, lb_ref, gate_ref, o_ref, xw_ref, y_ref):
    n_slabs, ts, slab = x_ref.shape
    c = n_slabs * slab
    i = pl.program_id(1)
    n_lane_blocks = c // V7X_LANES
    lbs_per_slab = slab // V7X_LANES

    def lanes_of(lb):
        return lb // lbs_per_slab, slice((lb % lbs_per_slab) * V7X_LANES, (lb % lbs_per_slab + 1) * V7X_LANES)

    for lb in range(n_lane_blocks):
        s, cols = lanes_of(lb)
        xw_ref[lb, CONV_HALO:CONV_HALO + ts, :] = x_ref[s, :, cols].astype(F32)

    @pl.when(i == 0)
    def _():
        xw_ref[:, 0:CONV_HALO, :] = jnp.zeros((n_lane_blocks, CONV_HALO, V7X_LANES), F32)

    @pl.when(i > 0)
    def _():
        for lb in range(n_lane_blocks):
            s, cols = lanes_of(lb)
            xw_ref[lb, 0:CONV_HALO, :] = halo_ref[s, :, cols].astype(F32)

    row_blk = 128
    sub = 8
    first_tap = CONV_HALO - (CONV_WIDTH - 1)

    def conv_body(it, carry):
        rb = it // n_lane_blocks
        lb = it % n_lane_blocks
        cols = pl.ds(pl.multiple_of(lb * V7X_LANES, V7X_LANES), V7X_LANES)
        base = rb * row_blk
        acc = [jnp.broadcast_to(cb_ref[:, cols], (sub, V7X_LANES))] * (row_blk // sub)
        for tap in range(CONV_WIDTH):
            wk = jnp.broadcast_to(w_ref[tap:tap + 1, cols], (sub, V7X_LANES))
            acc = [a + wk * xw_ref[lb, pl.ds(base + r * sub + first_tap + tap, sub), :]
                   for r, a in enumerate(acc)]
        for r, a in enumerate(acc):
            y_ref[pl.ds(pl.multiple_of(base + r * sub, sub), sub), cols] = a
        return carry

    lax.fori_loop(0, (ts // row_blk) * n_lane_blocks, conv_body, 0, unroll=2)

    ln_rows = 16

    def ln_body(r, carry):
        rows = pl.ds(pl.multiple_of(r * ln_rows, ln_rows), ln_rows)
        y = y_ref[rows, :]
        mu = jnp.mean(y, axis=-1, keepdims=True)
        d = y - mu
        var = jnp.mean(d * d, axis=-1, keepdims=True)
        z = (d * lax.rsqrt(var + EPS)) * lg_ref[...] + lb_ref[...]
        z = z * _sigmoid(z)
        gate = jnp.concatenate([gate_ref[s, rows, :] for s in range(n_slabs)], axis=1)
        o_ref[rows, :] = (z * gate.astype(F32)).astype(o_ref.dtype)
        return carry

    lax.fori_loop(0, ts // ln_rows, ln_body, 0, unroll=8)


def _conv_module(pa, conv_w, conv_b, ln_g, ln_b, *, batch, seq, w_b, x_col, gate_col, ts):
    slab = pa.shape[2]
    n_slabs = w_b // slab
    assert seq % ts == 0 and ts % 128 == 0 and ts % CONV_HALO == 0
    assert w_b % slab == 0 and x_col % w_b == 0 and gate_col % w_b == 0
    n_s = seq // ts
    halo_per_blk = ts // CONV_HALO
    return pl.pallas_call(
        _conv_module_kernel,
        grid=(batch, n_s),
        in_specs=[
            pl.BlockSpec((n_slabs, ts, slab), lambda b, i: (x_col // w_b, b * n_s + i, 0)),
            pl.BlockSpec((n_slabs, CONV_HALO, slab),
                         lambda b, i: (x_col // w_b, jnp.maximum((b * n_s + i) * halo_per_blk - 1, 0), 0)),
            pl.BlockSpec((CONV_WIDTH, w_b), lambda b, i: (0, 0)),
            pl.BlockSpec((1, w_b), lambda b, i: (0, 0)),
            pl.BlockSpec((1, w_b), lambda b, i: (0, 0)),
            pl.BlockSpec((1, w_b), lambda b, i: (0, 0)),
            pl.BlockSpec((n_slabs, ts, slab), lambda b, i: (gate_col // w_b, b * n_s + i, 0)),
        ],
        out_specs=pl.BlockSpec((ts, w_b), lambda b, i: (b * n_s + i, 0)),
        out_shape=jax.ShapeDtypeStruct((batch * seq, w_b), BF16),
        scratch_shapes=[pltpu.VMEM((w_b // V7X_LANES, CONV_HALO + ts, V7X_LANES), F32),
                        pltpu.VMEM((ts, w_b), F32)],
        compiler_params=_cparams("parallel", "arbitrary"),
        name="conv_module",
    )(pa, pa, conv_w.astype(F32), conv_b.reshape(1, w_b).astype(F32), ln_g.reshape(1, w_b).astype(F32),
      ln_b.reshape(1, w_b).astype(F32), pa)


def _sgu_kernel(ug_ref, v_ref, lg_ref, lb_ref, ws_ref, bs_ref, o_ref):
    tb, width = v_ref.shape
    gw = width // N_GROUPS_C
    pos_r = lax.broadcasted_iota(jnp.int32, (GMLP_CHUNK, GMLP_CHUNK), 0) // CHUNK
    pos_c = lax.broadcasted_iota(jnp.int32, (GMLP_CHUNK, GMLP_CHUNK), 1) // CHUNK
    causal = pos_r >= pos_c
    for n in range(tb // GMLP_CHUNK):
        rows = slice(n * GMLP_CHUNK, (n + 1) * GMLP_CHUNK)
        v = v_ref[rows, :].astype(F32)
        mu = jnp.mean(v, axis=-1, keepdims=True)
        d = v - mu
        var = jnp.mean(d * d, axis=-1, keepdims=True)
        vn = ((d * lax.rsqrt(var + EPS)) * lg_ref[...] + lb_ref[...]).astype(BF16)
        for g in range(N_GROUPS_C):
            cols = slice(g * gw, (g + 1) * gw)
            ws = jnp.where(causal, ws_ref[g], 0.0).astype(BF16)
            sg = _dot(ws, vn[:, cols]) + bs_ref[g]
            o_ref[rows, cols] = (ug_ref[rows, cols].astype(F32) * sg).astype(o_ref.dtype)


def _sgu(uv, ln_g, ln_b, w_s, b_s, *, width, tb):
    m = uv.shape[0]
    assert m % tb == 0 and tb % GMLP_CHUNK == 0
    return pl.pallas_call(
        _sgu_kernel,
        grid=(m // tb,),
        in_specs=[
            pl.BlockSpec((tb, width), lambda i: (i, 0)),
            pl.BlockSpec((tb, width), lambda i: (i, 1)),
            pl.BlockSpec((1, width), lambda i: (0, 0)),
            pl.BlockSpec((1, width), lambda i: (0, 0)),
            pl.BlockSpec((N_GROUPS_C, GMLP_CHUNK, GMLP_CHUNK), lambda i: (0, 0, 0)),
            pl.BlockSpec((N_GROUPS_C, GMLP_CHUNK, 1), lambda i: (0, 0, 0)),
        ],
        out_specs=pl.BlockSpec((tb, width), lambda i: (i, 0)),
        out_shape=jax.ShapeDtypeStruct((m, width), BF16),
        compiler_params=_cparams("parallel"),
        name="sgu",
    )(uv, uv, ln_g.reshape(1, width).astype(F32), ln_b.reshape(1, width).astype(F32),
      w_s.astype(F32), b_s.astype(F32)[..., None])


def _xattn_kernel(h_ref, g_ref, wq_ref, k_ref, v_ref, wo_ref, fg_ref, o_ref, hn_ref, q_ref, ctx_ref,
                  *, scale, final_norm):
    d = h_ref.shape[1]
    dh = d // N_HEADS_X
    x = h_ref[...]
    ms = jnp.mean(x * x, axis=-1, keepdims=True)
    hn_ref[...] = ((x * lax.rsqrt(ms + EPS)) * g_ref[...]).astype(BF16)
    q_ref[...] = (_dot(hn_ref[...], wq_ref[...]) * scale).astype(BF16)
    for h in range(N_HEADS_X):
        cols = slice(h * dh, (h + 1) * dh)
        s = _dot_nt(q_ref[:, cols], k_ref[:, cols])
        p = jnp.exp(s - jnp.max(s, axis=-1, keepdims=True))
        denom = jnp.sum(p, axis=-1, keepdims=True)
        ctx_ref[:, cols] = (_dot(p.astype(BF16), v_ref[:, cols]) / denom).astype(BF16)
    h_new = h_ref[...] + _dot(ctx_ref[...], wo_ref[...])
    if final_norm:
        ms = jnp.mean(h_new * h_new, axis=-1, keepdims=True)
        h_new = (h_new * lax.rsqrt(ms + EPS)) * fg_ref[...]
    o_ref[...] = h_new


def _cross_attention(h, g, wq, kv, wo, final_g, *, batch, seq, n_mem, tq, final_norm):
    m, d = h.shape
    assert seq % tq == 0
    n_s = seq // tq
    kern = functools.partial(_xattn_kernel, scale=(d // N_HEADS_X) ** -0.5, final_norm=final_norm)
    resident = dict(pipeline_mode=pl.Buffered(1))
    return pl.pallas_call(
        kern,
        grid=(batch, n_s),
        in_specs=[
            pl.BlockSpec((tq, d), lambda b, i: (b * n_s + i, 0)),
            pl.BlockSpec((1, d), lambda b, i: (0, 0)),
            pl.BlockSpec((d, d), lambda b, i: (0, 0), **resident),
            pl.BlockSpec((n_mem, d), lambda b, i: (b, 0)),
            pl.BlockSpec((n_mem, d), lambda b, i: (b, 1)),
            pl.BlockSpec((d, d), lambda b, i: (0, 0), **resident),
            pl.BlockSpec((1, d), lambda b, i: (0, 0)),
        ],
        out_specs=pl.BlockSpec((tq, d), lambda b, i: (b * n_s + i, 0)),
        out_shape=jax.ShapeDtypeStruct((m, d), F32),
        scratch_shapes=[pltpu.VMEM((tq, d), BF16), pltpu.VMEM((tq, d), BF16), pltpu.VMEM((tq, d), BF16)],
        compiler_params=_cparams("parallel", "arbitrary"),
        name="cross_attention",
    )(h, g.reshape(1, d).astype(F32), wq, kv, kv, wo, final_g.reshape(1, d).astype(F32))


def _pick(n, *cands):
    for c in cands:
        if n % c == 0:
            return c
    raise ValueError(f"no tile for {n} among {cands}")


def kernel(x, mem, norm_mix_g, norm_x_g, norm_mem_g, final_norm_g, w_in_ab, rel_bias, conv_w, conv_b,
           conv_ln_g, conv_ln_b, w_out_ab, w_in_c, sgu_ln_g, sgu_ln_b, w_s, b_s, w_out_c, w_xq, w_xk,
           w_xv, w_xo):
    batch, seq, d = x.shape
    n_mem = mem.shape[1]
    depth = norm_mix_g.shape[0]
    mix = w_out_ab.shape[1]
    w_a = mix // 2
    w_b = mix - w_a
    t = batch * seq
    tm = _pick(t, 1024, 512, 256)
    tn = _pick(w_a, 1024, 512, 256)
    tm_out = _pick(t, 512, 256)
    nb_a, nb_b, nb_mix, nb_d = w_a // tn, w_b // tn, mix // tn, d // tn
    scale_a = HEAD_DIM_A ** -0.5

    h = x.reshape(t, d).astype(F32)
    mem2 = mem.reshape(batch * n_mem, d).astype(F32)
    for layer in range(depth):
        i = layer // 2
        if layer % 2 == 0:
            segs = [
                (nb_a, "plain", scale_a, 0, None),
                (2 * nb_a, "plain", 1.0, nb_a, None),
                (nb_b, "glu", 1.0, 3 * nb_a, 3 * nb_a + nb_b),
                (nb_mix, "silu", 1.0, 3 * nb_a + 2 * nb_b, None),
            ]
            pa = _norm_matmul(h, norm_mix_g[layer], w_in_ab[i].astype(BF16), segs, tm=tm, tn=tn, slab=ATT_SLAB)
            gate_col = 3 * w_a + w_b
            bias = _bias_tables(rel_bias[i])
            ya = _band_attention(pa, bias, batch=batch, seq=seq, w_a=w_a, gate_col=gate_col)
            yb = _conv_module(pa, conv_w[i], conv_b[i], conv_ln_g[i], conv_ln_b[i], batch=batch, seq=seq,
                              w_b=w_b, x_col=3 * w_a, gate_col=gate_col + w_a, ts=_pick(seq, 512, 256, 128))
            h = _resid_matmul(ya, 0, yb, 0, w_out_ab[i].astype(BF16), h, tm=tm_out, tn=d)
        else:
            segs = [
                (nb_mix, "mulsilu", 1.0, 0, 2 * nb_mix),
                (nb_mix, "plain", 1.0, nb_mix, None),
            ]
            uv = _norm_matmul(h, norm_mix_g[layer], w_in_c[i].astype(BF16), segs, tm=tm, tn=tn)
            y = _sgu(uv, sgu_ln_g[i], sgu_ln_b[i], w_s[i], b_s[i], width=mix, tb=_pick(t, 512, 256, 128))
            h = _resid_matmul(y, 0, y, 1, w_out_c[i].astype(BF16), h, tm=tm_out, tn=d)

        w_kv = jnp.concatenate([w_xk[layer], w_xv[layer]], axis=1).astype(BF16)
        kv = _norm_matmul(mem2, norm_mem_g[layer], w_kv, [(2 * nb_d, "plain", 1.0, 0, None)],
                          tm=_pick(batch * n_mem, 512, 256, 128), tn=tn)
        h = _cross_attention(h, norm_x_g[layer], w_xq[layer].astype(BF16), kv, w_xo[layer].astype(BF16),
                             final_norm_g, batch=batch, seq=seq, n_mem=n_mem, tq=_pick(seq, 512, 256, 128),
                             final_norm=(layer == depth - 1))
    return h.reshape(batch, seq, d)
```

```python
import functools

import jax
import jax.numpy as jnp
from jax import lax
from jax.experimental import pallas as pl
from jax.experimental.pallas import tpu as pltpu

F32 = jnp.float32
BF16 = jnp.bfloat16

CHUNK = 64
N_PAST_CHUNKS = 8
MAX_REL = 128
HEAD_DIM_A = 128
CONV_WIDTH = 31
GMLP_CHUNK = 128
N_GROUPS_C = 8
N_HEADS_X = 4
EPS = 1e-6
NEG_INF = -1e30

V7X_LANES = 128
V7X_VMEM_LIMIT_BYTES = 56 * 1024 * 1024

ATT_QB = 2 * CHUNK
ATT_WIN = N_PAST_CHUNKS * CHUNK + ATT_QB
ATT_ROLL_WIDTH = 768
ATT_N_VARIANTS = N_PAST_CHUNKS * CHUNK // ATT_QB + 1
REL_ROWS = 384
CONV_HALO = 32
ATT_SLAB = 2 * HEAD_DIM_A


def _cparams(*sem):
    return pltpu.CompilerParams(dimension_semantics=sem, vmem_limit_bytes=V7X_VMEM_LIMIT_BYTES)


def _sigmoid(x):
    return 0.5 * jnp.tanh(0.5 * x) + 0.5


def _dot(a, b):
    return jnp.dot(a, b, preferred_element_type=F32)


def _dot_nt(a, b):
    return lax.dot_general(a, b, (((1,), (1,)), ((), ())), preferred_element_type=F32)


def _norm_matmul_kernel(a_tab, b_tab, x_ref, g_ref, wa_ref, wb_ref, o_ref, hn_ref, *, segments):
    del a_tab, b_tab
    j = pl.program_id(1)
    tm = x_ref.shape[0]

    def project(kind, scale, rows):
        lhs = hn_ref[rows, :]
        a = _dot(lhs, wa_ref[...])
        if kind == "plain":
            r = a if scale == 1.0 else a * scale
        elif kind == "silu":
            r = a * _sigmoid(a)
        elif kind == "glu":
            r = a * _sigmoid(_dot(lhs, wb_ref[...]))
        elif kind == "mulsilu":
            b = _dot(lhs, wb_ref[...])
            r = a * (b * _sigmoid(b))
        else:
            raise ValueError(kind)
        if len(o_ref.shape) == 2:
            o_ref[rows, :] = r.astype(o_ref.dtype)
        else:
            slab = o_ref.shape[2]
            for c in range(o_ref.shape[0]):
                o_ref[c, rows, :] = r[:, c * slab:(c + 1) * slab].astype(o_ref.dtype)

    @pl.when(j == 0)
    def _():
        n_chunks = 4
        rows_c = tm // n_chunks
        for c in range(n_chunks):
            rows = slice(c * rows_c, (c + 1) * rows_c)
            x = x_ref[rows, :].astype(F32)
            ms = jnp.mean(x * x, axis=-1, keepdims=True)
            hn_ref[rows, :] = ((x * lax.rsqrt(ms + EPS)) * g_ref[...]).astype(BF16)
            project(segments[0][1], segments[0][2], rows)

    lo = 0
    for n_steps, kind, scale in segments:
        hi = lo + n_steps

        @pl.when((j >= max(lo, 1)) & (j < hi))
        def _(kind=kind, scale=scale):
            project(kind, scale, slice(0, tm))

        lo = hi


def _norm_matmul(x, g, w, segments, *, tm, tn, slab=None):
    m, k = x.shape
    assert m % tm == 0 and w.shape[0] == k and w.shape[1] % tn == 0
    a_tab, b_tab = [], []
    for n_steps, _, _, a0, b0 in segments:
        for s in range(n_steps):
            a_tab.append(a0 + s)
            b_tab.append(None if b0 is None else b0 + s)
    known = [b for b in b_tab if b is not None]
    fill = known[0] if known else 0
    for idx, b in enumerate(b_tab):
        if b is None:
            b_tab[idx] = fill
        else:
            fill = b
    n_steps_total = len(a_tab)
    kern = functools.partial(_norm_matmul_kernel, segments=tuple(s[:3] for s in segments))
    if slab is None:
        out_spec = pl.BlockSpec((tm, tn), lambda i, j, at, bt: (i, j))
        out_shape = jax.ShapeDtypeStruct((m, n_steps_total * tn), BF16)
    else:
        assert tn % slab == 0
        out_spec = pl.BlockSpec((tn // slab, tm, slab), lambda i, j, at, bt: (j, i, 0))
        out_shape = jax.ShapeDtypeStruct((n_steps_total * tn // slab, m, slab), BF16)
    grid_spec = pltpu.PrefetchScalarGridSpec(
        num_scalar_prefetch=2,
        grid=(m // tm, n_steps_total),
        in_specs=[
            pl.BlockSpec((tm, k), lambda i, j, at, bt: (i, 0)),
            pl.BlockSpec((1, k), lambda i, j, at, bt: (0, 0)),
            pl.BlockSpec((k, tn), lambda i, j, at, bt: (0, at[j])),
            pl.BlockSpec((k, tn), lambda i, j, at, bt: (0, bt[j])),
        ],
        out_specs=out_spec,
        scratch_shapes=[pltpu.VMEM((tm, k), BF16)],
    )
    return pl.pallas_call(
        kern,
        grid_spec=grid_spec,
        out_shape=out_shape,
        compiler_params=_cparams("parallel", "arbitrary"),
        name="norm_matmul",
    )(jnp.asarray(a_tab, jnp.int32), jnp.asarray(b_tab, jnp.int32), x, g.reshape(1, k), w, w)


def _resid_matmul_kernel(l0_ref, l1_ref, w0_ref, w1_ref, r_ref, o_ref):
    acc = _dot(l0_ref[...], w0_ref[...]) + _dot(l1_ref[...], w1_ref[...])
    o_ref[...] = r_ref[...] + acc


def _resid_matmul(l0, c0, l1, c1, w, resid, *, tm, tn):
    m, n = resid.shape
    kh = w.shape[0] // 2
    assert m % tm == 0 and n % tn == 0
    resident = dict(pipeline_mode=pl.Buffered(1)) if tn == n else {}
    return pl.pallas_call(
        _resid_matmul_kernel,
        grid=(m // tm, n // tn),
        in_specs=[
            pl.BlockSpec((tm, kh), lambda i, j: (i, c0)),
            pl.BlockSpec((tm, kh), lambda i, j: (i, c1)),
            pl.BlockSpec((kh, tn), lambda i, j: (0, j), **resident),
            pl.BlockSpec((kh, tn), lambda i, j: (1, j), **resident),
            pl.BlockSpec((tm, tn), lambda i, j: (i, j)),
        ],
        out_specs=pl.BlockSpec((tm, tn), lambda i, j: (i, j)),
        out_shape=jax.ShapeDtypeStruct((m, n), F32),
        compiler_params=_cparams("parallel", "arbitrary"),
        name="resid_matmul",
    )(l0, l1, w, w, resid)


def _bias_table_kernel(rb_ref, o_ref, *, n_heads):
    t = pl.program_id(0)
    width = ATT_ROLL_WIDTH
    m = lax.broadcasted_iota(jnp.int32, (REL_ROWS, width), 1)
    r = lax.broadcasted_iota(jnp.int32, (REL_ROWS, width), 0)
    mm = jnp.where(m < ATT_WIN, m, m - width)
    idx = jnp.clip(ATT_QB * t - mm, -MAX_REL, MAX_REL) + MAX_REL
    onehot = jnp.where(r == idx, 1.0, 0.0).astype(BF16)
    rb = rb_ref[...]
    hi = rb.astype(BF16)
    rem = rb - hi.astype(F32)
    mid = rem.astype(BF16)
    lo = (rem - mid.astype(F32)).astype(BF16)
    u = _dot(hi, onehot) + _dot(mid, onehot) + _dot(lo, onehot)

    qi = lax.broadcasted_iota(jnp.int32, (ATT_QB, ATT_WIN), 0) // CHUNK
    kj = lax.broadcasted_iota(jnp.int32, (ATT_QB, ATT_WIN), 1) // CHUNK
    q_chunk = (ATT_QB // CHUNK) * t + qi
    valid = (kj <= q_chunk) & (kj >= q_chunk - N_PAST_CHUNKS)
    for h in range(n_heads):
        x = jnp.broadcast_to(u[h:h + 1, :], (ATT_QB, width))
        toeplitz = pltpu.roll(x, 0, 1, stride=1, stride_axis=0)
        o_ref[0, h] = jnp.where(valid, toeplitz[:, :ATT_WIN], NEG_INF)


def _bias_tables(rel_bias):
    n_heads, n_rel = rel_bias.shape
    rb = jnp.pad(rel_bias.astype(F32), ((0, 0), (0, REL_ROWS - n_rel)))
    return pl.pallas_call(
        functools.partial(_bias_table_kernel, n_heads=n_heads),
        grid=(ATT_N_VARIANTS,),
        in_specs=[pl.BlockSpec((n_heads, REL_ROWS), lambda t: (0, 0))],
        out_specs=pl.BlockSpec((1, n_heads, ATT_QB, ATT_WIN), lambda t: (t, 0, 0, 0)),
        out_shape=jax.ShapeDtypeStruct((ATT_N_VARIANTS, n_heads, ATT_QB, ATT_WIN), F32),
        compiler_params=_cparams("arbitrary"),
        name="bias_tables",
    )(rb)


def _band_attn_kernel(q_ref, k_ref, v_ref, bias_ref, gate_ref, o_ref,
                      s_even, s_odd, p_even, p_odd, l_even, l_odd, *, heads_per_step):
    dh = HEAD_DIM_A
    n_blocks = q_ref.shape[0] // ATT_QB
    last_variant = ATT_N_VARIANTS - 1
    heads = [(h, slice(h * dh, (h + 1) * dh)) for h in range(heads_per_step)]

    def window(blk):
        return pl.ds(pl.multiple_of(jnp.maximum(blk - last_variant, 0) * ATT_QB, ATT_QB), ATT_WIN)

    def block_rows(blk):
        return pl.ds(pl.multiple_of(blk * ATT_QB, ATT_QB), ATT_QB)

    def scores(blk, s_buf):
        variant = jnp.minimum(blk, last_variant)
        for h, cols in heads:
            s_buf[h] = _dot_nt(q_ref[block_rows(blk), cols], k_ref[window(blk), cols]) + bias_ref[variant, h]

    def exponentiate(s_buf, p_buf, l_buf):
        for h, _ in heads:
            s = s_buf[h]
            p = jnp.exp(s - jnp.max(s, axis=-1, keepdims=True))
            l_buf[h] = jnp.sum(p, axis=-1, keepdims=True)
            p_buf[h] = p.astype(BF16)

    def apply_values(blk, p_buf, l_buf):
        rows = block_rows(blk)
        for h, cols in heads:
            o = _dot(p_buf[h], v_ref[window(blk), cols]) / l_buf[h]
            o_ref[rows, cols] = (o * gate_ref[rows, cols].astype(F32)).astype(o_ref.dtype)

    scores(0, s_even)
    scores(1, s_odd)
    exponentiate(s_even, p_even, l_even)

    def pair(t, carry):
        m = 2 * t + 1
        scores(m + 1, s_even)
        exponentiate(s_odd, p_odd, l_odd)
        apply_values(m - 1, p_even, l_even)
        scores(m + 2, s_odd)
        exponentiate(s_even, p_even, l_even)
        apply_values(m, p_odd, l_odd)
        return carry

    lax.fori_loop(0, (n_blocks - 2) // 2, pair, 0)
    exponentiate(s_odd, p_odd, l_odd)
    apply_values(n_blocks - 2, p_even, l_even)
    apply_values(n_blocks - 1, p_odd, l_odd)


def _band_attention(pa, bias, *, batch, seq, w_a, gate_col):
    heads_per_step = ATT_SLAB // HEAD_DIM_A
    cw = ATT_SLAB
    n_blocks = seq // ATT_QB
    assert seq % ATT_QB == 0 and n_blocks % 2 == 0 and n_blocks >= 4
    assert w_a % cw == 0 and gate_col % cw == 0 and pa.shape[2] == cw
    kern = functools.partial(_band_attn_kernel, heads_per_step=heads_per_step)
    seq_cols = lambda first: pl.BlockSpec((None, seq, cw), lambda b, g: (first // cw + g, b, 0))
    return pl.pallas_call(
        kern,
        grid=(batch, w_a // cw),
        in_specs=[
            seq_cols(0),
            seq_cols(w_a),
            seq_cols(2 * w_a),
            pl.BlockSpec((ATT_N_VARIANTS, heads_per_step, ATT_QB, ATT_WIN), lambda b, g: (0, g, 0, 0)),
            seq_cols(gate_col),
        ],
        out_specs=pl.BlockSpec((seq, cw), lambda b, g: (b, g)),
        out_shape=jax.ShapeDtypeStruct((batch * seq, w_a), BF16),
        scratch_shapes=[
            pltpu.VMEM((heads_per_step, ATT_QB, ATT_WIN), F32),
            pltpu.VMEM((heads_per_step, ATT_QB, ATT_WIN), F32),
            pltpu.VMEM((heads_per_step, ATT_QB, ATT_WIN), BF16),
            pltpu.VMEM((heads_per_step, ATT_QB, ATT_WIN), BF16),
            pltpu.VMEM((heads_per_step, ATT_QB, 1), F32),
            pltpu.VMEM((heads_per_step, ATT_QB, 1), F32),
        ],
        compiler_params=_cparams("parallel", "parallel"),
        name="band_attention",
    )(pa, pa, pa, bias, pa)


def _conv_module_kernel(x_ref, halo_ref, w_ref, cb_ref, lg_ref, lb_ref, gate_ref, o_ref, xw_ref, y_ref):
    n_slabs, ts, slab = x_ref.shape
    c = n_slabs * slab
    i = pl.program_id(1)
    n_lane_blocks = c // V7X_LANES
    lbs_per_slab = slab // V7X_LANES

    def lanes_of(lb):
        return lb // lbs_per_slab, slice((lb % lbs_per_slab) * V7X_LANES, (lb % lbs_per_slab + 1) * V7X_LANES)

    for lb in range(n_lane_blocks):
        s, cols = lanes_of(lb)
        xw_ref[lb, CONV_HALO:CONV_HALO + ts, :] = x_ref[s, :, cols].astype(F32)

    @pl.when(i == 0)
    def _():
        xw_ref[:, 0:CONV_HALO, :] = jnp.zeros((n_lane_blocks, CONV_HALO, V7X_LANES), F32)

    @pl.when(i > 0)
    def _():
        for lb in range(n_lane_blocks):
            s, cols = lanes_of(lb)
            xw_ref[lb, 0:CONV_HALO, :] = halo_ref[s, :, cols].astype(F32)

    row_blk = 128
    sub = 8
    first_tap = CONV_HALO - (CONV_WIDTH - 1)

    def conv_body(it, carry):
        rb = it // n_lane_blocks
        lb = it % n_lane_blocks
        cols = pl.ds(pl.multiple_of(lb * V7X_LANES, V7X_LANES), V7X_LANES)
        base = rb * row_blk
        acc = [jnp.broadcast_to(cb_ref[:, cols], (sub, V7X_LANES))] * (row_blk // sub)
        for tap in range(CONV_WIDTH):
            wk = jnp.broadcast_to(w_ref[tap:tap + 1, cols], (sub, V7X_LANES))
            acc = [a + wk * xw_ref[lb, pl.ds(base + r * sub + first_tap + tap, sub), :]
                   for r, a in enumerate(acc)]
        for r, a in enumerate(acc):
            y_ref[pl.ds(pl.multiple_of(base + r * sub, sub), sub), cols] = a
        return carry

    lax.fori_loop(0, (ts // row_blk) * n_lane_blocks, conv_body, 0, unroll=2)

    ln_rows = 16

    def ln_body(r, carry):
        rows = pl.ds(pl.multiple_of(r * ln_rows, ln_rows), ln_rows)
        y = y_ref[rows, :]
        mu = jnp.mean(y, axis=-1, keepdims=True)
        d = y - mu
        var = jnp.mean(d * d, axis=-1, keepdims=True)
        z = (d * lax.rsqrt(var + EPS)) * lg_ref[...] + lb_ref[...]
        z = z * _sigmoid(z)
        gate = jnp.concatenate([gate_ref[s, rows, :] for s in range(n_slabs)], axis=1)
        o_ref[rows, :] = (z * gate.astype(F32)).astype(o_ref.dtype)
        return carry

    lax.fori_loop(0, ts // ln_rows, ln_body, 0, unroll=8)


def _conv_module(pa, conv_w, conv_b, ln_g, ln_b, *, batch, seq, w_b, x_col, gate_col, ts):
    slab = pa.shape[2]
    n_slabs = w_b // slab
    assert seq % ts == 0 and ts % 128 == 0 and ts % CONV_HALO == 0
    assert w_b % slab == 0 and x_col % w_b == 0 and gate_col % w_b == 0
    n_s = seq // ts
    halo_per_blk = ts // CONV_HALO
    return pl.pallas_call(
        _conv_module_kernel,
        grid=(batch, n_s),
        in_specs=[
            pl.BlockSpec((n_slabs, ts, slab), lambda b, i: (x_col // w_b, b * n_s + i, 0)),
            pl.BlockSpec((n_slabs, CONV_HALO, slab),
                         lambda b, i: (x_col // w_b, jnp.maximum((b * n_s + i) * halo_per_blk - 1, 0), 0)),
            pl.BlockSpec((CONV_WIDTH, w_b), lambda b, i: (0, 0)),
            pl.BlockSpec((1, w_b), lambda b, i: (0, 0)),
            pl.BlockSpec((1, w_b), lambda b, i: (0, 0)),
            pl.BlockSpec((1, w_b), lambda b, i: (0, 0)),
            pl.BlockSpec((n_slabs, ts, slab), lambda b, i: (gate_col // w_b, b * n_s + i, 0)),
        ],
        out_specs=pl.BlockSpec((ts, w_b), lambda b, i: (b * n_s + i, 0)),
        out_shape=jax.ShapeDtypeStruct((batch * seq, w_b), BF16),
        scratch_shapes=[pltpu.VMEM((w_b // V7X_LANES, CONV_HALO + ts, V7X_LANES), F32),
                        pltpu.VMEM((ts, w_b), F32)],
        compiler_params=_cparams("parallel", "arbitrary"),
        name="conv_module",
    )(pa, pa, conv_w.astype(F32), conv_b.reshape(1, w_b).astype(F32), ln_g.reshape(1, w_b).astype(F32),
      ln_b.reshape(1, w_b).astype(F32), pa)


def _sgu_kernel(ug_ref, v_ref, lg_ref, lb_ref, ws_ref, bs_ref, o_ref):
    tb, width = v_ref.shape
    gw = width // N_GROUPS_C
    pos_r = lax.broadcasted_iota(jnp.int32, (GMLP_CHUNK, GMLP_CHUNK), 0) // CHUNK
    pos_c = lax.broadcasted_iota(jnp.int32, (GMLP_CHUNK, GMLP_CHUNK), 1) // CHUNK
    causal = pos_r >= pos_c
    for n in range(tb // GMLP_CHUNK):
        rows = slice(n * GMLP_CHUNK, (n + 1) * GMLP_CHUNK)
        v = v_ref[rows, :].astype(F32)
        mu = jnp.mean(v, axis=-1, keepdims=True)
        d = v - mu
        var = jnp.mean(d * d, axis=-1, keepdims=True)
        vn = ((d * lax.rsqrt(var + EPS)) * lg_ref[...] + lb_ref[...]).astype(BF16)
        for g in range(N_GROUPS_C):
            cols = slice(g * gw, (g + 1) * gw)
            ws = jnp.where(causal, ws_ref[g], 0.0).astype(BF16)
            sg = _dot(ws, vn[:, cols]) + bs_ref[g]
            o_ref[rows, cols] = (ug_ref[rows, cols].astype(F32) * sg).astype(o_ref.dtype)


def _sgu(uv, ln_g, ln_b, w_s, b_s, *, width, tb):
    m = uv.shape[0]
    assert m % tb == 0 and tb % GMLP_CHUNK == 0
    return pl.pallas_call(
        _sgu_kernel,
        grid=(m // tb,),
        in_specs=[
            pl.BlockSpec((tb, width), lambda i: (i, 0)),
            pl.BlockSpec((tb, width), lambda i: (i, 1)),
            pl.BlockSpec((1, width), lambda i: (0, 0)),
            pl.BlockSpec((1, width), lambda i: (0, 0)),
            pl.BlockSpec((N_GROUPS_C, GMLP_CHUNK, GMLP_CHUNK), lambda i: (0, 0, 0)),
            pl.BlockSpec((N_GROUPS_C, GMLP_CHUNK, 1), lambda i: (0, 0, 0)),
        ],
        out_specs=pl.BlockSpec((tb, width), lambda i: (i, 0)),
        out_shape=jax.ShapeDtypeStruct((m, width), BF16),
        compiler_params=_cparams("parallel"),
        name="sgu",
    )(uv, uv, ln_g.reshape(1, width).astype(F32), ln_b.reshape(1, width).astype(F32),
      w_s.astype(F32), b_s.astype(F32)[..., None])


def _xattn_kernel(h_ref, g_ref, wq_ref, k_ref, v_ref, wo_ref, fg_ref, o_ref, hn_ref, q_ref, ctx_ref,
                  *, scale, final_norm):
    d = h_ref.shape[1]
    dh = d // N_HEADS_X
    x = h_ref[...]
    ms = jnp.mean(x * x, axis=-1, keepdims=True)
    hn_ref[...] = ((x * lax.rsqrt(ms + EPS)) * g_ref[...]).astype(BF16)
    q_ref[...] = (_dot(hn_ref[...], wq_ref[...]) * scale).astype(BF16)
    for h in range(N_HEADS_X):
        cols = slice(h * dh, (h + 1) * dh)
        s = _dot_nt(q_ref[:, cols], k_ref[:, cols])
        p = jnp.exp(s - jnp.max(s, axis=-1, keepdims=True))
        denom = jnp.sum(p, axis=-1, keepdims=True)
        ctx_ref[:, cols] = (_dot(p.astype(BF16), v_ref[:, cols]) / denom).astype(BF16)
    h_new = h_ref[...] + _dot(ctx_ref[...], wo_ref[...])
    if final_norm:
        ms = jnp.mean(h_new * h_new, axis=-1, keepdims=True)
        h_new = (h_new * lax.rsqrt(ms + EPS)) * fg_ref[...]
    o_ref[...] = h_new


def _cross_attention(h, g, wq, kv, wo, final_g, *, batch, seq, n_mem, tq, final_norm):
    m, d = h.shape
    assert seq % tq == 0
    n_s = seq // tq
    kern = functools.partial(_xattn_kernel, scale=(d // N_HEADS_X) ** -0.5, final_norm=final_norm)
    resident = dict(pipeline_mode=pl.Buffered(1))
    return pl.pallas_call(
        kern,
        grid=(batch, n_s),
        in_specs=[
            pl.BlockSpec((tq, d), lambda b, i: (b * n_s + i, 0)),
            pl.BlockSpec((1, d), lambda b, i: (0, 0)),
            pl.BlockSpec((d, d), lambda b, i: (0, 0), **resident),
            pl.BlockSpec((n_mem, d), lambda b, i: (b, 0)),
            pl.BlockSpec((n_mem, d), lambda b, i: (b, 1)),
            pl.BlockSpec((d, d), lambda b, i: (0, 0), **resident),
            pl.BlockSpec((1, d), lambda b, i: (0, 0)),
        ],
        out_specs=pl.BlockSpec((tq, d), lambda b, i: (b * n_s + i, 0)),
        out_shape=jax.ShapeDtypeStruct((m, d), F32),
        scratch_shapes=[pltpu.VMEM((tq, d), BF16), pltpu.VMEM((tq, d), BF16), pltpu.VMEM((tq, d), BF16)],
        compiler_params=_cparams("parallel", "arbitrary"),
        name="cross_attention",
    )(h, g.reshape(1, d).astype(F32), wq, kv, kv, wo, final_g.reshape(1, d).astype(F32))


def _pick(n, *cands):
    for c in cands:
        if n % c == 0:
            return c
    raise ValueError(f"no tile for {n} among {cands}")


def kernel(x, mem, norm_mix_g, norm_x_g, norm_mem_g, final_norm_g, w_in_ab, rel_bias, conv_w, conv_b,
           conv_ln_g, conv_ln_b, w_out_ab, w_in_c, sgu_ln_g, sgu_ln_b, w_s, b_s, w_out_c, w_xq, w_xk,
           w_xv, w_xo):
    batch, seq, d = x.shape
    n_mem = mem.shape[1]
    depth = norm_mix_g.shape[0]
    mix = w_out_ab.shape[1]
    w_a = mix // 2
    w_b = mix - w_a
    t = batch * seq
    tm = _pick(t, 1024, 512, 256)
    tn = _pick(w_a, 1024, 512, 256)
    tm_out = _pick(t, 512, 256)
    nb_a, nb_b, nb_mix, nb_d = w_a // tn, w_b // tn, mix // tn, d // tn
    scale_a = HEAD_DIM_A ** -0.5

    h = x.reshape(t, d).astype(F32)
    mem2 = mem.reshape(batch * n_mem, d).astype(F32)
    for layer in range(depth):
        i = layer // 2
        if layer % 2 == 0:
            segs = [
                (nb_a, "plain", scale_a, 0, None),
                (2 * nb_a, "plain", 1.0, nb_a, None),
                (nb_b, "glu", 1.0, 3 * nb_a, 3 * nb_a + nb_b),
                (nb_mix, "silu", 1.0, 3 * nb_a + 2 * nb_b, None),
            ]
            pa = _norm_matmul(h, norm_mix_g[layer], w_in_ab[i].astype(BF16), segs, tm=tm, tn=tn, slab=ATT_SLAB)
            gate_col = 3 * w_a + w_b
            bias = _bias_tables(rel_bias[i])
            ya = _band_attention(pa, bias, batch=batch, seq=seq, w_a=w_a, gate_col=gate_col)
            yb = _conv_module(pa, conv_w[i], conv_b[i], conv_ln_g[i], conv_ln_b[i], batch=batch, seq=seq,
                              w_b=w_b, x_col=3 * w_a, gate_col=gate_col + w_a, ts=_pick(seq, 512, 256, 128))
            h = _resid_matmul(ya, 0, yb, 0, w_out_ab[i].astype(BF16), h, tm=tm_out, tn=d)
        else:
            segs = [
                (nb_mix, "mulsilu", 1.0, 0, 2 * nb_mix),
                (nb_mix, "plain", 1.0, nb_mix, None),
            ]
            uv = _norm_matmul(h, norm_mix_g[layer], w_in_c[i].astype(BF16), segs, tm=tm, tn=tn)
            y = _sgu(uv, sgu_ln_g[i], sgu_ln_b[i], w_s[i], b_s[i], width=mix, tb=_pick(t, 512, 256, 128))
            h = _resid_matmul(y, 0, y, 1, w_out_c[i].astype(BF16), h, tm=tm_out, tn=d)

        w_kv = jnp.concatenate([w_xk[layer], w_xv[layer]], axis=1).astype(BF16)
        kv = _norm_matmul(mem2, norm_mem_g[layer], w_kv, [(2 * nb_d, "plain", 1.0, 0, None)],
                          tm=_pick(batch * n_mem, 512, 256, 128), tn=tn)
        h = _cross_attention(h, norm_x_g[layer], w_xq[layer].astype(BF16), kv, w_xo[layer].astype(BF16),
                             final_norm_g, batch=batch, seq=seq, n_mem=n_mem, tq=_pick(seq, 512, 256, 128),
                             final_norm=(layer == depth - 1))
    return h.reshape(batch, seq, d)
```

```python
import functools

import jax
import jax.numpy as jnp
from jax import lax
from jax.experimental import pallas as pl
from jax.experimental.pallas import tpu as pltpu

F32 = jnp.float32
BF16 = jnp.bfloat16

CHUNK = 64
N_PAST_CHUNKS = 8
MAX_REL = 128
HEAD_DIM_A = 128
CONV_WIDTH = 31
GMLP_CHUNK = 128
N_GROUPS_C = 8
N_HEADS_X = 4
EPS = 1e-6
NEG_INF = -1e30

V7X_LANES = 128
V7X_VMEM_LIMIT_BYTES = 56 * 1024 * 1024

ATT_QB = 2 * CHUNK
ATT_WIN = N_PAST_CHUNKS * CHUNK + ATT_QB
ATT_ROLL_WIDTH = 768
ATT_N_VARIANTS = N_PAST_CHUNKS * CHUNK // ATT_QB + 1
REL_ROWS = 384
CONV_HALO = 32
ATT_SLAB = 2 * HEAD_DIM_A


def _cparams(*sem):
    return pltpu.CompilerParams(dimension_semantics=sem, vmem_limit_bytes=V7X_VMEM_LIMIT_BYTES)


def _sigmoid(x):
    return 0.5 * jnp.tanh(0.5 * x) + 0.5


def _dot(a, b):
    return jnp.dot(a, b, preferred_element_type=F32)


def _dot_nt(a, b):
    return lax.dot_general(a, b, (((1,), (1,)), ((), ())), preferred_element_type=F32)


def _norm_matmul_kernel(a_tab, b_tab, x_ref, g_ref, wa_ref, wb_ref, o_ref, hn_ref, *, segments):
    del a_tab, b_tab
    j = pl.program_id(1)
    tm = x_ref.shape[0]

    def project(kind, scale, rows):
        lhs = hn_ref[rows, :]
        a = _dot(lhs, wa_ref[...])
        if kind == "plain":
            r = a if scale == 1.0 else a * scale
        elif kind == "silu":
            r = a * _sigmoid(a)
        elif kind == "glu":
            r = a * _sigmoid(_dot(lhs, wb_ref[...]))
        elif kind == "mulsilu":
            b = _dot(lhs, wb_ref[...])
            r = a * (b * _sigmoid(b))
        else:
            raise ValueError(kind)
        if len(o_ref.shape) == 2:
            o_ref[rows, :] = r.astype(o_ref.dtype)
        else:
            slab = o_ref.shape[2]
            for c in range(o_ref.shape[0]):
                o_ref[c, rows, :] = r[:, c * slab:(c + 1) * slab].astype(o_ref.dtype)

    @pl.when(j == 0)
    def _():
        n_chunks = 4
        rows_c = tm // n_chunks
        for c in range(n_chunks):
            rows = slice(c * rows_c, (c + 1) * rows_c)
            x = x_ref[rows, :].astype(F32)
            ms = jnp.mean(x * x, axis=-1, keepdims=True)
            hn_ref[rows, :] = ((x * lax.rsqrt(ms + EPS)) * g_ref[...]).astype(BF16)
            project(segments[0][1], segments[0][2], rows)

    lo = 0
    for n_steps, kind, scale in segments:
        hi = lo + n_steps

        @pl.when((j >= max(lo, 1)) & (j < hi))
        def _(kind=kind, scale=scale):
            project(kind, scale, slice(0, tm))

        lo = hi


def _norm_matmul(x, g, w, segments, *, tm, tn, slab=None):
    m, k = x.shape
    assert m % tm == 0 and w.shape[0] == k and w.shape[1] % tn == 0
    a_tab, b_tab = [], []
    for n_steps, _, _, a0, b0 in segments:
        for s in range(n_steps):
            a_tab.append(a0 + s)
            b_tab.append(None if b0 is None else b0 + s)
    known = [b for b in b_tab if b is not None]
    fill = known[0] if known else 0
    for idx, b in enumerate(b_tab):
        if b is None:
            b_tab[idx] = fill
        else:
            fill = b
    n_steps_total = len(a_tab)
    kern = functools.partial(_norm_matmul_kernel, segments=tuple(s[:3] for s in segments))
    if slab is None:
        out_spec = pl.BlockSpec((tm, tn), lambda i, j, at, bt: (i, j))
        out_shape = jax.ShapeDtypeStruct((m, n_steps_total * tn), BF16)
    else:
        assert tn % slab == 0
        out_spec = pl.BlockSpec((tn // slab, tm, slab), lambda i, j, at, bt: (j, i, 0))
        out_shape = jax.ShapeDtypeStruct((n_steps_total * tn // slab, m, slab), BF16)
    grid_spec = pltpu.PrefetchScalarGridSpec(
        num_scalar_prefetch=2,
        grid=(m // tm, n_steps_total),
        in_specs=[
            pl.BlockSpec((tm, k), lambda i, j, at, bt: (i, 0)),
            pl.BlockSpec((1, k), lambda i, j, at, bt: (0, 0)),
            pl.BlockSpec((k, tn), lambda i, j, at, bt: (0, at[j])),
            pl.BlockSpec((k, tn), lambda i, j, at, bt: (0, bt[j])),
        ],
        out_specs=out_spec,
        scratch_shapes=[pltpu.VMEM((tm, k), BF16)],
    )
    return pl.pallas_call(
        kern,
        grid_spec=grid_spec,
        out_shape=out_shape,
        compiler_params=_cparams("parallel", "arbitrary"),
        name="norm_matmul",
    )(jnp.asarray(a_tab, jnp.int32), jnp.asarray(b_tab, jnp.int32), x, g.reshape(1, k), w, w)


def _resid_matmul_kernel(l0_ref, l1_ref, w0_ref, w1_ref, r_ref, o_ref):
    acc = _dot(l0_ref[...], w0_ref[...]) + _dot(l1_ref[...], w1_ref[...])
    o_ref[...] = r_ref[...] + acc


def _resid_matmul(l0, c0, l1, c1, w, resid, *, tm, tn):
    m, n = resid.shape
    kh = w.shape[0] // 2
    assert m % tm == 0 and n % tn == 0
    resident = dict(pipeline_mode=pl.Buffered(1)) if tn == n else {}
    return pl.pallas_call(
        _resid_matmul_kernel,
        grid=(m // tm, n // tn),
        in_specs=[
            pl.BlockSpec((tm, kh), lambda i, j: (i, c0)),
            pl.BlockSpec((tm, kh), lambda i, j: (i, c1)),
            pl.BlockSpec((kh, tn), lambda i, j: (0, j), **resident),
            pl.BlockSpec((kh, tn), lambda i, j: (1, j), **resident),
            pl.BlockSpec((tm, tn), lambda i, j: (i, j)),
        ],
        out_specs=pl.BlockSpec((tm, tn), lambda i, j: (i, j)),
        out_shape=jax.ShapeDtypeStruct((m, n), F32),
        compiler_params=_cparams("parallel", "arbitrary"),
        name="resid_matmul",
    )(l0, l1, w, w, resid)


def _bias_table_kernel(rb_ref, o_ref, *, n_heads):
    t = pl.program_id(0)
    width = ATT_ROLL_WIDTH
    m = lax.broadcasted_iota(jnp.int32, (REL_ROWS, width), 1)
    r = lax.broadcasted_iota(jnp.int32, (REL_ROWS, width), 0)
    mm = jnp.where(m < ATT_WIN, m, m - width)
    idx = jnp.clip(ATT_QB * t - mm, -MAX_REL, MAX_REL) + MAX_REL
    onehot = jnp.where(r == idx, 1.0, 0.0).astype(BF16)
    rb = rb_ref[...]
    hi = rb.astype(BF16)
    rem = rb - hi.astype(F32)
    mid = rem.astype(BF16)
    lo = (rem - mid.astype(F32)).astype(BF16)
    u = _dot(hi, onehot) + _dot(mid, onehot) + _dot(lo, onehot)

    qi = lax.broadcasted_iota(jnp.int32, (ATT_QB, ATT_WIN), 0) // CHUNK
    kj = lax.broadcasted_iota(jnp.int32, (ATT_QB, ATT_WIN), 1) // CHUNK
    q_chunk = (ATT_QB // CHUNK) * t + qi
    valid = (kj <= q_chunk) & (kj >= q_chunk - N_PAST_CHUNKS)
    for h in range(n_heads):
        x = jnp.broadcast_to(u[h:h + 1, :], (ATT_QB, width))
        toeplitz = pltpu.roll(x, 0, 1, stride=1, stride_axis=0)
        o_ref[0, h] = jnp.where(valid, toeplitz[:, :ATT_WIN], NEG_INF)


def _bias_tables(rel_bias):
    n_heads, n_rel = rel_bias.shape
    rb = jnp.pad(rel_bias.astype(F32), ((0, 0), (0, REL_ROWS - n_rel)))
    return pl.pallas_call(
        functools.partial(_bias_table_kernel, n_heads=n_heads),
        grid=(ATT_N_VARIANTS,),
        in_specs=[pl.BlockSpec((n_heads, REL_ROWS), lambda t: (0, 0))],
        out_specs=pl.BlockSpec((1, n_heads, ATT_QB, ATT_WIN), lambda t: (t, 0, 0, 0)),
        out_shape=jax.ShapeDtypeStruct((ATT_N_VARIANTS, n_heads, ATT_QB, ATT_WIN), F32),
        compiler_params=_cparams("arbitrary"),
        name="bias_tables",
    )(rb)


def _band_attn_kernel(q_ref, k_ref, v_ref, bias_ref, gate_ref, o_ref,
                      s_even, s_odd, p_even, p_odd, l_even, l_odd, *, heads_per_step):
    dh = HEAD_DIM_A
    n_blocks = q_ref.shape[0] // ATT_QB
    last_variant = ATT_N_VARIANTS - 1
    heads = [(h, slice(h * dh, (h + 1) * dh)) for h in range(heads_per_step)]

    def window(blk):
        return pl.ds(pl.multiple_of(jnp.maximum(blk - last_variant, 0) * ATT_QB, ATT_QB), ATT_WIN)

    def block_rows(blk):
        return pl.ds(pl.multiple_of(blk * ATT_QB, ATT_QB), ATT_QB)

    def scores(blk, s_buf):
        variant = jnp.minimum(blk, last_variant)
        for h, cols in heads:
            s_buf[h] = _dot_nt(q_ref[block_rows(blk), cols], k_ref[window(blk), cols]) + bias_ref[variant, h]

    def exponentiate(s_buf, p_buf, l_buf):
        for h, _ in heads:
            s = s_buf[h]
            p = jnp.exp(s - jnp.max(s, axis=-1, keepdims=True))
            l_buf[h] = jnp.sum(p, axis=-1, keepdims=True)
            p_buf[h] = p.astype(BF16)

    def apply_values(blk, p_buf, l_buf):
        rows = block_rows(blk)
        for h, cols in heads:
            o = _dot(p_buf[h], v_ref[window(blk), cols]) / l_buf[h]
            o_ref[rows, cols] = (o * gate_ref[rows, cols].astype(F32)).astype(o_ref.dtype)

    scores(0, s_even)
    scores(1, s_odd)
    exponentiate(s_even, p_even, l_even)

    def pair(t, carry):
        m = 2 * t + 1
        scores(m + 1, s_even)
        exponentiate(s_odd, p_odd, l_odd)
        apply_values(m - 1, p_even, l_even)
        scores(m + 2, s_odd)
        exponentiate(s_even, p_even, l_even)
        apply_values(m, p_odd, l_odd)
        return carry

    lax.fori_loop(0, (n_blocks - 2) // 2, pair, 0)
    exponentiate(s_odd, p_odd, l_odd)
    apply_values(n_blocks - 2, p_even, l_even)
    apply_values(n_blocks - 1, p_odd, l_odd)


def _band_attention(pa, bias, *, batch, seq, w_a, gate_col):
    heads_per_step = ATT_SLAB // HEAD_DIM_A
    cw = ATT_SLAB
    n_blocks = seq // ATT_QB
    assert seq % ATT_QB == 0 and n_blocks % 2 == 0 and n_blocks >= 4
    assert w_a % cw == 0 and gate_col % cw == 0 and pa.shape[2] == cw
    kern = functools.partial(_band_attn_kernel, heads_per_step=heads_per_step)
    seq_cols = lambda first: pl.BlockSpec((None, seq, cw), lambda b, g: (first // cw + g, b, 0))
    return pl.pallas_call(
        kern,
        grid=(batch, w_a // cw),
        in_specs=[
            seq_cols(0),
            seq_cols(w_a),
            seq_cols(2 * w_a),
            pl.BlockSpec((ATT_N_VARIANTS, heads_per_step, ATT_QB, ATT_WIN), lambda b, g: (0, g, 0, 0)),
            seq_cols(gate_col),
        ],
        out_specs=pl.BlockSpec((seq, cw), lambda b, g: (b, g)),
        out_shape=jax.ShapeDtypeStruct((batch * seq, w_a), BF16),
        scratch_shapes=[
            pltpu.VMEM((heads_per_step, ATT_QB, ATT_WIN), F32),
            pltpu.VMEM((heads_per_step, ATT_QB, ATT_WIN), F32),
            pltpu.VMEM((heads_per_step, ATT_QB, ATT_WIN), BF16),
            pltpu.VMEM((heads_per_step, ATT_QB, ATT_WIN), BF16),
            pltpu.VMEM((heads_per_step, ATT_QB, 1), F32),
            pltpu.VMEM((heads_per_step, ATT_QB, 1), F32),
        ],
        compiler_params=_cparams("parallel", "parallel"),
        name="band_attention",
    )(pa, pa, pa, bias, pa)


def _conv_module_kernel(x_ref, halo_ref, w_ref, cb_ref, lg_ref, lb_ref, gate_ref, o_ref, xw_ref, y_ref):
    n_slabs, ts, slab = x_ref.shape
    c = n_slabs * slab
    i = pl.program_id(1)
    n_lane_blocks = c // V7X_LANES
    lbs_per_slab = slab // V7X_LANES

    def lanes_of(lb):
        return lb // lbs_per_slab, slice((lb % lbs_per_slab) * V7X_LANES, (lb % lbs_per_slab + 1) * V7X_LANES)

    for lb in range(n_lane_blocks):
        s, cols = lanes_of(lb)
        xw_ref[lb, CONV_HALO:CONV_HALO + ts, :] = x_ref[s, :, cols].astype(F32)

    @pl.when(i == 0)
    def _():
        xw_ref[:, 0:CONV_HALO, :] = jnp.zeros((n_lane_blocks, CONV_HALO, V7X_LANES), F32)

    @pl.when(i > 0)
    def _():
        for lb in range(n_lane_blocks):
            s, cols = lanes_of(lb)
            xw_ref[lb, 0:CONV_HALO, :] = halo_ref[s, :, cols].astype(F32)

    row_blk = 128
    sub = 8
    first_tap = CONV_HALO - (CONV_WIDTH - 1)

    def conv_body(it, carry):
        rb = it // n_lane_blocks
        lb = it % n_lane_blocks
        cols = pl.ds(pl.multiple_of(lb * V7X_LANES, V7X_LANES), V7X_LANES)
        base = rb * row_blk
        acc = [jnp.broadcast_to(cb_ref[:, cols], (sub, V7X_LANES))] * (row_blk // sub)
        for tap in range(CONV_WIDTH):
            wk = jnp.broadcast_to(w_ref[tap:tap + 1, cols], (sub, V7X_LANES))
            acc = [a + wk * xw_ref[lb, pl.ds(base + r * sub + first_tap + tap, sub), :]
                   for r, a in enumerate(acc)]
        for r, a in enumerate(acc):
            y_ref[pl.ds(pl.multiple_of(base + r * sub, sub), sub), cols] = a
        return carry

    lax.fori_loop(0, (ts // row_blk) * n_lane_blocks, conv_body, 0, unroll=2)

    ln_rows = 16

    def ln_body(r, carry):
        rows = pl.ds(pl.multiple_of(r * ln_rows, ln_rows), ln_rows)
        y = y_ref[rows, :]
        mu = jnp.mean(y, axis=-1, keepdims=True)
        d = y - mu
        var = jnp.mean(d * d, axis=-1, keepdims=True)
        z = (d * lax.rsqrt(var + EPS)) * lg_ref[...] + lb_ref[...]
        z = z * _sigmoid(z)
        gate = jnp.concatenate([gate_ref[s, rows, :] for s in range(n_slabs)], axis=1)
        o_ref[rows, :] = (z * gate.astype(F32)).astype(o_ref.dtype)
        return carry

    lax.fori_loop(0, ts // ln_rows, ln_body, 0, unroll=8)


def _conv_module(pa, conv_w, conv_b, ln_g, ln_b, *, batch, seq, w_b, x_col, gate_col, ts):
    slab = pa.shape[2]
    n_slabs = w_b // slab
    assert seq % ts == 0 and ts % 128 == 0 and ts % CONV_HALO == 0
    assert w_b % slab == 0 and x_col % w_b == 0 and gate_col % w_b == 0
    n_s = seq // ts
    halo_per_blk = ts // CONV_HALO
    return pl.pallas_call(
        _conv_module_kernel,
        grid=(batch, n_s),
        in_specs=[
            pl.BlockSpec((n_slabs, ts, slab), lambda b, i: (x_col // w_b, b * n_s + i, 0)),
            pl.BlockSpec((n_slabs, CONV_HALO, slab),
                         lambda b, i: (x_col // w_b, jnp.maximum((b * n_s + i) * halo_per_blk - 1, 0), 0)),
            pl.BlockSpec((CONV_WIDTH, w_b), lambda b, i: (0, 0)),
            pl.BlockSpec((1, w_b), lambda b, i: (0, 0)),
            pl.BlockSpec((1, w_b), lambda b, i: (0, 0)),
            pl.BlockSpec((1, w_b), lambda b, i: (0, 0)),
            pl.BlockSpec((n_slabs, ts, slab), lambda b, i: (gate_col // w_b, b * n_s + i, 0)),
        ],
        out_specs=pl.BlockSpec((ts, w_b), lambda b, i: (b * n_s + i, 0)),
        out_shape=jax.ShapeDtypeStruct((batch * seq, w_b), BF16),
        scratch_shapes=[pltpu.VMEM((w_b // V7X_LANES, CONV_HALO + ts, V7X_LANES), F32),
                        pltpu.VMEM((ts, w_b), F32)],
        compiler_params=_cparams("parallel", "arbitrary"),
        name="conv_module",
    )(pa, pa, conv_w.astype(F32), conv_b.reshape(1, w_b).astype(F32), ln_g.reshape(1, w_b).astype(F32),
      ln_b.reshape(1, w_b).astype(F32), pa)


def _sgu_kernel(ug_ref, v_ref, lg_ref, lb_ref, ws_ref, bs_ref, o_ref):
    tb, width = v_ref.shape
    gw = width // N_GROUPS_C
    pos_r = lax.broadcasted_iota(jnp.int32, (GMLP_CHUNK, GMLP_CHUNK), 0) // CHUNK
    pos_c = lax.broadcasted_iota(jnp.int32, (GMLP_CHUNK, GMLP_CHUNK), 1) // CHUNK
    causal = pos_r >= pos_c
    ws_masked = [jnp.where(causal, ws_ref[g], 0.0).astype(BF16) for g in range(N_GROUPS_C)]
    for n in range(tb // GMLP_CHUNK):
        rows = slice(n * GMLP_CHUNK, (n + 1) * GMLP_CHUNK)
        v = v_ref[rows, :].astype(F32)
        mu = jnp.mean(v, axis=-1, keepdims=True)
        d = v - mu
        var = jnp.mean(d * d, axis=-1, keepdims=True)
        vn = ((d * lax.rsqrt(var + EPS)) * lg_ref[...] + lb_ref[...]).astype(BF16)
        for g in range(N_GROUPS_C):
            cols = slice(g * gw, (g + 1) * gw)
            sg = _dot(ws_masked[g], vn[:, cols]) + bs_ref[g]
            o_ref[rows, cols] = (ug_ref[rows, cols].astype(F32) * sg).astype(o_ref.dtype)


def _sgu(uv, ln_g, ln_b, w_s, b_s, *, width, tb):
    m = uv.shape[0]
    assert m % tb == 0 and tb % GMLP_CHUNK == 0
    return pl.pallas_call(
        _sgu_kernel,
        grid=(m // tb,),
        in_specs=[
            pl.BlockSpec((tb, width), lambda i: (i, 0)),
            pl.BlockSpec((tb, width), lambda i: (i, 1)),
            pl.BlockSpec((1, width), lambda i: (0, 0)),
            pl.BlockSpec((1, width), lambda i: (0, 0)),
            pl.BlockSpec((N_GROUPS_C, GMLP_CHUNK, GMLP_CHUNK), lambda i: (0, 0, 0)),
            pl.BlockSpec((N_GROUPS_C, GMLP_CHUNK, 1), lambda i: (0, 0, 0)),
        ],
        out_specs=pl.BlockSpec((tb, width), lambda i: (i, 0)),
        out_shape=jax.ShapeDtypeStruct((m, width), BF16),
        compiler_params=_cparams("parallel"),
        name="sgu",
    )(uv, uv, ln_g.reshape(1, width).astype(F32), ln_b.reshape(1, width).astype(F32),
      w_s.astype(F32), b_s.astype(F32)[..., None])


def _xattn_kernel(h_ref, g_ref, wq_ref, k_ref, v_ref, wo_ref, fg_ref, o_ref, hn_ref, q_ref, ctx_ref,
                  *, scale, final_norm):
    d = h_ref.shape[1]
    dh = d // N_HEADS_X
    x = h_ref[...]
    ms = jnp.mean(x * x, axis=-1, keepdims=True)
    hn_ref[...] = ((x * lax.rsqrt(ms + EPS)) * g_ref[...]).astype(BF16)
    q_ref[...] = (_dot(hn_ref[...], wq_ref[...]) * scale).astype(BF16)
    for h in range(N_HEADS_X):
        cols = slice(h * dh, (h + 1) * dh)
        s = _dot_nt(q_ref[:, cols], k_ref[:, cols])
        p = jnp.exp(s - jnp.max(s, axis=-1, keepdims=True))
        denom = jnp.sum(p, axis=-1, keepdims=True)
        ctx_ref[:, cols] = (_dot(p.astype(BF16), v_ref[:, cols]) / denom).astype(BF16)
    h_new = h_ref[...] + _dot(ctx_ref[...], wo_ref[...])
    if final_norm:
        ms = jnp.mean(h_new * h_new, axis=-1, keepdims=True)
        h_new = (h_new * lax.rsqrt(ms + EPS)) * fg_ref[...]
    o_ref[...] = h_new


def _cross_attention(h, g, wq, kv, wo, final_g, *, batch, seq, n_mem, tq, final_norm):
    m, d = h.shape
    assert seq % tq == 0
    n_s = seq // tq
    kern = functools.partial(_xattn_kernel, scale=(d // N_HEADS_X) ** -0.5, final_norm=final_norm)
    resident = dict(pipeline_mode=pl.Buffered(1))
    return pl.pallas_call(
        kern,
        grid=(batch, n_s),
        in_specs=[
            pl.BlockSpec((tq, d), lambda b, i: (b * n_s + i, 0)),
            pl.BlockSpec((1, d), lambda b, i: (0, 0)),
            pl.BlockSpec((d, d), lambda b, i: (0, 0), **resident),
            pl.BlockSpec((n_mem, d), lambda b, i: (b, 0)),
            pl.BlockSpec((n_mem, d), lambda b, i: (b, 1)),
            pl.BlockSpec((d, d), lambda b, i: (0, 0), **resident),
            pl.BlockSpec((1, d), lambda b, i: (0, 0)),
        ],
        out_specs=pl.BlockSpec((tq, d), lambda b, i: (b * n_s + i, 0)),
        out_shape=jax.ShapeDtypeStruct((m, d), F32),
        scratch_shapes=[pltpu.VMEM((tq, d), BF16), pltpu.VMEM((tq, d), BF16), pltpu.VMEM((tq, d), BF16)],
        compiler_params=_cparams("parallel", "arbitrary"),
        name="cross_attention",
    )(h, g.reshape(1, d).astype(F32), wq, kv, kv, wo, final_g.reshape(1, d).astype(F32))


def _pick(n, *cands):
    for c in cands:
        if n % c == 0:
            return c
    raise ValueError(f"no tile for {n} among {cands}")


def kernel(x, mem, norm_mix_g, norm_x_g, norm_mem_g, final_norm_g, w_in_ab, rel_bias, conv_w, conv_b,
           conv_ln_g, conv_ln_b, w_out_ab, w_in_c, sgu_ln_g, sgu_ln_b, w_s, b_s, w_out_c, w_xq, w_xk,
           w_xv, w_xo):
    batch, seq, d = x.shape
    n_mem = mem.shape[1]
    depth = norm_mix_g.shape[0]
    mix = w_out_ab.shape[1]
    w_a = mix // 2
    w_b = mix - w_a
    t = batch * seq
    tm = _pick(t, 1024, 512, 256)
    tn = _pick(w_a, 1024, 512, 256)
    tm_out = _pick(t, 512, 256)
    nb_a, nb_b, nb_mix, nb_d = w_a // tn, w_b // tn, mix // tn, d // tn
    scale_a = HEAD_DIM_A ** -0.5

    h = x.reshape(t, d).astype(F32)
    mem2 = mem.reshape(batch * n_mem, d).astype(F32)
    for layer in range(depth):
        i = layer // 2
        if layer % 2 == 0:
            segs = [
                (nb_a, "plain", scale_a, 0, None),
                (2 * nb_a, "plain", 1.0, nb_a, None),
                (nb_b, "glu", 1.0, 3 * nb_a, 3 * nb_a + nb_b),
                (nb_mix, "silu", 1.0, 3 * nb_a + 2 * nb_b, None),
            ]
            pa = _norm_matmul(h, norm_mix_g[layer], w_in_ab[i].astype(BF16), segs, tm=tm, tn=tn, slab=ATT_SLAB)
            gate_col = 3 * w_a + w_b
            bias = _bias_tables(rel_bias[i])
            ya = _band_attention(pa, bias, batch=batch, seq=seq, w_a=w_a, gate_col=gate_col)
            yb = _conv_module(pa, conv_w[i], conv_b[i], conv_ln_g[i], conv_ln_b[i], batch=batch, seq=seq,
                              w_b=w_b, x_col=3 * w_a, gate_col=gate_col + w_a, ts=_pick(seq, 512, 256, 128))
            h = _resid_matmul(ya, 0, yb, 0, w_out_ab[i].astype(BF16), h, tm=tm_out, tn=d)
        else:
            segs = [
                (nb_mix, "mulsilu", 1.0, 0, 2 * nb_mix),
                (nb_mix, "plain", 1.0, nb_mix, None),
            ]
            uv = _norm_matmul(h, norm_mix_g[layer], w_in_c[i].astype(BF16), segs, tm=tm, tn=tn)
            y = _sgu(uv, sgu_ln_g[i], sgu_ln_b[i], w_s[i], b_s[i], width=mix, tb=_pick(t, 512, 256, 128))
            h = _resid_matmul(y, 0, y, 1, w_out_c[i].astype(BF16), h, tm=tm_out, tn=d)

        w_kv = jnp.concatenate([w_xk[layer], w_xv[layer]], axis=1).astype(BF16)
        kv = _norm_matmul(mem2, norm_mem_g[layer], w_kv, [(2 * nb_d, "plain", 1.0, 0, None)],
                          tm=_pick(batch * n_mem, 512, 256, 128), tn=tn)
        h = _cross_attention(h, norm_x_g[layer], w_xq[layer].astype(BF16), kv, w_xo[layer].astype(BF16),
                             final_norm_g, batch=batch, seq=seq, n_mem=n_mem, tq=_pick(seq, 512, 256, 128),
                             final_norm=(layer == depth - 1))
    return h.reshape(batch, seq, d)
```
